```python
import jax, jax.numpy as jnp
from jax import lax
import numpy as np


D_MODEL = 2048
BATCH = 4
SEQ = 8192
DEPTH = 2
DEC_BATCH = 8
DEC_SEQ = 4096
PAST_LEN = 128

GRID_W = 64
HEAD_DIM = 128
NA_HEADS = 8
NA_ROWS = 8
NA_COLS = 16
GA_HEADS = 8
GA_KV_HEADS = 2
ROPE_THETA = 10000.0
SW_HEADS = 8
SW_KV_HEADS = 2
WINDOW = 128
Q_BLOCK = 128
N_BRANCH = 3
D_FF = 5632
EPS = 1e-6

NA_W = NA_HEADS * HEAD_DIM
GA_QW = GA_HEADS * HEAD_DIM
GA_KW = GA_KV_HEADS * HEAD_DIM
SW_QW = SW_HEADS * HEAD_DIM
SW_KW = SW_KV_HEADS * HEAD_DIM
BRANCH_W = NA_W
D_IN = 3 * NA_W + GA_QW + 2 * GA_KW + SW_QW + 2 * SW_KW

kernel_name = 'hybrid_na_gqa_swa_macaron_encoder'


def rms_norm(x, g):
    xf = x.astype(jnp.float32)
    y = xf * lax.rsqrt(jnp.mean(xf * xf, axis=-1, keepdims=True) + EPS)
    return (y * g.astype(jnp.float32)).astype(x.dtype)


def swiglu(x, wg, wu, wd):
    return (jax.nn.silu(x @ wg) * (x @ wu)) @ wd


def alibi_slopes(n):
    return 2.0 ** (-8.0 * jnp.arange(1, n + 1, dtype=jnp.float32) / n)


def axial_rope(x, row, col):
    half = HEAD_DIM // 2
    quarter = half // 2
    freqs = ROPE_THETA ** (-jnp.arange(quarter, dtype=jnp.float32) / quarter)

    def rot(part, pos):
        ang = pos.astype(jnp.float32)[:, None] * freqs[None, :]
        c = jnp.cos(ang)[None, :, None, :]
        s = jnp.sin(ang)[None, :, None, :]
        p1, p2 = part[..., :quarter], part[..., quarter:]
        return jnp.concatenate([p1 * c - p2 * s, p2 * c + p1 * s], axis=-1)

    xf = x.astype(jnp.float32)
    out = jnp.concatenate([rot(xf[..., :half], row), rot(xf[..., half:], col)], axis=-1)
    return out.astype(x.dtype)


def neighborhood_attention(q, k, v, rel_bias):
    B, S, H, hd = q.shape
    rows = S // GRID_W
    kh = min(NA_ROWS, rows)
    kw = NA_COLS
    scale = hd ** -0.5
    q = q.reshape(B, rows, GRID_W, H, hd)
    k = k.reshape(B, rows, GRID_W, H, hd)
    v = v.reshape(B, rows, GRID_W, H, hd)
    cq = jnp.arange(GRID_W)
    col_idx = jnp.clip(cq - kw // 2, 0, GRID_W - kw)[:, None] + jnp.arange(kw)[None, :]
    col_bias_idx = col_idx - cq[:, None] + NA_COLS - 1
    table_c = rel_bias.astype(jnp.float32)[:, :, col_bias_idx]

    def one_row(r):
        rs = jnp.clip(r - kh // 2, 0, rows - kh)
        row_bias_idx = rs + jnp.arange(kh) - r + NA_ROWS - 1
        bias = table_c[:, row_bias_idx].transpose(0, 2, 1, 3)
        ks = lax.dynamic_slice_in_dim(k, rs, kh, axis=1)[:, :, col_idx]
        vs = lax.dynamic_slice_in_dim(v, rs, kh, axis=1)[:, :, col_idx]
        qr = lax.dynamic_index_in_dim(q, r, axis=1, keepdims=False)
        s = jnp.einsum('bqhd,bkqjhd->bhqkj', qr, ks).astype(jnp.float32) * scale + bias[None]
        p = jax.nn.softmax(s.reshape(B, H, GRID_W, kh * kw), axis=-1)
        p = p.reshape(B, H, GRID_W, kh, kw).astype(v.dtype)
        return jnp.einsum('bhqkj,bkqjhd->bqhd', p, vs)

    out = lax.map(one_row, jnp.arange(rows))
    return out.transpose(1, 0, 2, 3, 4).reshape(B, S, H * hd)


def global_attention(q, k, v):
    B, S, Hq, hd = q.shape
    Hkv = k.shape[2]
    G = Hq // Hkv
    nblk = S // Q_BLOCK
    scale = hd ** -0.5
    qb = q.reshape(B, nblk, Q_BLOCK, Hkv, G, hd).transpose(1, 0, 2, 3, 4, 5)

    def blk(qi):
        s = jnp.einsum('bqkgd,bskd->bkgqs', qi, k).astype(jnp.float32) * scale
        p = jax.nn.softmax(s, axis=-1).astype(v.dtype)
        return jnp.einsum('bkgqs,bskd->bqkgd', p, v)

    out = lax.map(blk, qb)
    return out.transpose(1, 0, 2, 3, 4, 5).reshape(B, S, Hq * hd)


def sliding_window_attention(q, k, v, sink):
    B, S, Hq, hd = q.shape
    Hkv = k.shape[2]
    G = Hq // Hkv
    nblk = S // Q_BLOCK
    span = Q_BLOCK + 2 * WINDOW
    scale = hd ** -0.5
    pad = ((0, 0), (WINDOW, WINDOW), (0, 0), (0, 0))
    kp = jnp.pad(k, pad)
    vp = jnp.pad(v, pad)
    qb = q.reshape(B, nblk, Q_BLOCK, Hkv, G, hd).transpose(1, 0, 2, 3, 4, 5)
    slopes = alibi_slopes(Hq).reshape(Hkv, G)
    sink_f = sink.astype(jnp.float32).reshape(Hkv, G)

    def blk(args):
        i, qi = args
        start = i * Q_BLOCK
        ks = lax.dynamic_slice_in_dim(kp, start, span, axis=1)
        vs = lax.dynamic_slice_in_dim(vp, start, span, axis=1)
        qpos = start + jnp.arange(Q_BLOCK)
        kpos = start - WINDOW + jnp.arange(span)
        dist = jnp.abs(qpos[:, None] - kpos[None, :])
        valid = (dist <= WINDOW) & (kpos >= 0)[None, :] & (kpos < S)[None, :]
        s = jnp.einsum('bqkgd,bskd->bkgqs', qi, ks).astype(jnp.float32) * scale
        s = s - slopes[:, :, None, None] * dist.astype(jnp.float32)[None, None]
        s = jnp.where(valid[None, None, None], s, -jnp.inf)
        sink_col = jnp.broadcast_to(sink_f[None, :, :, None, None], s.shape[:-1] + (1,))
        p = jax.nn.softmax(jnp.concatenate([s, sink_col], axis=-1), axis=-1)[..., :-1]
        return jnp.einsum('bkgqs,bskd->bqkgd', p.astype(v.dtype), vs)

    out = lax.map(blk, (jnp.arange(nblk), qb))
    return out.transpose(1, 0, 2, 3, 4, 5).reshape(B, S, Hq * hd)


def token_mixer(h, w_in, w_gate, na_q_norm, na_k_norm, na_rel_bias, ga_q_norm, ga_k_norm,
                sw_q_norm, sw_k_norm, sw_sink, w_branch, w_out):
    B, S, _ = h.shape
    widths = (NA_W, NA_W, NA_W, GA_QW, GA_KW, GA_KW, SW_QW, SW_KW, SW_KW)
    cuts = []
    acc = 0
    for wdt in widths[:-1]:
        acc += wdt
        cuts.append(acc)
    qa, ka, va, qb, kb, vb, qc, kc, vc = jnp.split(h @ w_in, cuts, axis=-1)

    def heads(t, n):
        return t.reshape(B, S, n, HEAD_DIM)

    t = jnp.arange(S)
    row = t // GRID_W
    col = t % GRID_W
    o_a = neighborhood_attention(rms_norm(heads(qa, NA_HEADS), na_q_norm),
                                 rms_norm(heads(ka, NA_HEADS), na_k_norm),
                                 heads(va, NA_HEADS), na_rel_bias)
    o_b = global_attention(axial_rope(rms_norm(heads(qb, GA_HEADS), ga_q_norm), row, col),
                           axial_rope(rms_norm(heads(kb, GA_KV_HEADS), ga_k_norm), row, col),
                           heads(vb, GA_KV_HEADS))
    o_c = sliding_window_attention(rms_norm(heads(qc, SW_HEADS), sw_q_norm),
                                   rms_norm(heads(kc, SW_KV_HEADS), sw_k_norm),
                                   heads(vc, SW_KV_HEADS), sw_sink)
    merged = None
    for i, o in enumerate((o_a, o_b, o_c)):
        term = jax.nn.sigmoid(h @ w_gate[i]) * (o @ w_branch[i])
        merged = term if merged is None else merged + term
    return merged @ w_out


def trunk(x, ffn1_norm, ffn1_wg, ffn1_wu, ffn1_wd, mix_norm, w_in, w_gate, na_q_norm, na_k_norm,
          na_rel_bias, ga_q_norm, ga_k_norm, sw_q_norm, sw_k_norm, sw_sink, w_branch, w_out,
          ffn2_norm, ffn2_wg, ffn2_wu, ffn2_wd):
    for l in range(DEPTH):
        x = x + 0.5 * swiglu(rms_norm(x, ffn1_norm[l]), ffn1_wg[l], ffn1_wu[l], ffn1_wd[l])
        x = x + token_mixer(rms_norm(x, mix_norm[l]), w_in[l], w_gate[l], na_q_norm[l], na_k_norm[l],
                            na_rel_bias[l], ga_q_norm[l], ga_k_norm[l], sw_q_norm[l], sw_k_norm[l],
                            sw_sink[l], w_branch[l], w_out[l])
        x = x + 0.5 * swiglu(rms_norm(x, ffn2_norm[l]), ffn2_wg[l], ffn2_wu[l], ffn2_wd[l])
    return x


def setup_inputs(seed: int = 0) -> dict:
    key = jax.random.key(seed)
    ks = iter(jax.random.split(key, 32))

    def nrm(shape, scale):
        return jax.random.normal(next(ks), shape, jnp.float32) * scale

    def gain(shape):
        return 1.0 + 0.02 * jax.random.normal(next(ks), shape, jnp.float32)

    d = D_MODEL
    return dict(
        x_prompt=nrm((BATCH, SEQ, d), 1.0),
        x_sample=nrm((DEC_BATCH, DEC_SEQ, d), 1.0),
        ffn1_norm=gain((DEPTH, d)),
        ffn1_wg=nrm((DEPTH, d, D_FF), d ** -0.5),
        ffn1_wu=nrm((DEPTH, d, D_FF), d ** -0.5),
        ffn1_wd=nrm((DEPTH, D_FF, d), D_FF ** -0.5),
        mix_norm=gain((DEPTH, d)),
        w_in=nrm((DEPTH, d, D_IN), d ** -0.5),
        w_gate=nrm((DEPTH, N_BRANCH, d, d), d ** -0.5),
        na_q_norm=gain((DEPTH, HEAD_DIM)),
        na_k_norm=gain((DEPTH, HEAD_DIM)),
        na_rel_bias=nrm((DEPTH, NA_HEADS, 2 * NA_ROWS - 1, 2 * NA_COLS - 1), 0.5),
        ga_q_norm=gain((DEPTH, HEAD_DIM)),
        ga_k_norm=gain((DEPTH, HEAD_DIM)),
        sw_q_norm=gain((DEPTH, HEAD_DIM)),
        sw_k_norm=gain((DEPTH, HEAD_DIM)),
        sw_sink=nrm((DEPTH, SW_HEADS), 1.0),
        w_branch=nrm((DEPTH, N_BRANCH, BRANCH_W, d), BRANCH_W ** -0.5),
        w_out=nrm((DEPTH, d, d), d ** -0.5),
        ffn2_norm=gain((DEPTH, d)),
        ffn2_wg=nrm((DEPTH, d, D_FF), d ** -0.5),
        ffn2_wu=nrm((DEPTH, d, D_FF), d ** -0.5),
        ffn2_wd=nrm((DEPTH, D_FF, d), D_FF ** -0.5),
    )


def reference(x_prompt, x_sample, ffn1_norm, ffn1_wg, ffn1_wu, ffn1_wd, mix_norm, w_in, w_gate,
              na_q_norm, na_k_norm, na_rel_bias, ga_q_norm, ga_k_norm, sw_q_norm, sw_k_norm,
              sw_sink, w_branch, w_out, ffn2_norm, ffn2_wg, ffn2_wu, ffn2_wd):
    y_prompt = trunk(x_prompt, ffn1_norm, ffn1_wg, ffn1_wu, ffn1_wd, mix_norm, w_in, w_gate,
                     na_q_norm, na_k_norm, na_rel_bias, ga_q_norm, ga_k_norm, sw_q_norm, sw_k_norm,
                     sw_sink, w_branch, w_out, ffn2_norm, ffn2_wg, ffn2_wu, ffn2_wd)
    y_sample = trunk(x_sample, ffn1_norm, ffn1_wg, ffn1_wu, ffn1_wd, mix_norm, w_in, w_gate,
                     na_q_norm, na_k_norm, na_rel_bias, ga_q_norm, ga_k_norm, sw_q_norm, sw_k_norm,
                     sw_sink, w_branch, w_out, ffn2_norm, ffn2_wg, ffn2_wu, ffn2_wd)
    return (y_prompt, y_sample)
```

```python
import functools
import math

import numpy as np
import jax
import jax.numpy as jnp
from jax import lax
from jax.experimental import pallas as pl
from jax.experimental.pallas import tpu as pltpu

F32 = jnp.float32
BF16 = jnp.bfloat16

EPS = 1e-6
HEAD_DIM = 128
GRID_W = 64
NA_HEADS = 8
NA_ROWS = 8
NA_COLS = 16
GA_HEADS = 8
GA_KV_HEADS = 2
SW_HEADS = 8
SW_KV_HEADS = 2
WINDOW = 128
Q_BLOCK = 128
ROPE_THETA = 10000.0
MASKED = -1e30

NA_Q0, NA_K0, NA_V0 = 0, 8, 16
GA_Q0, GA_K0, GA_V0 = 24, 32, 34
SW_Q0, SW_K0, SW_V0 = 36, 44, 46
N_IN_HEADS = 48
D_IN = N_IN_HEADS * HEAD_DIM

QKV_SEGMENTS = (
    (NA_Q0, 8, True, False), (NA_K0, 8, True, False), (NA_V0, 8, False, False),
    (GA_Q0, 8, True, True), (GA_K0, 2, True, True), (GA_V0, 2, False, False),
    (SW_Q0, 8, True, False), (SW_K0, 2, True, False), (SW_V0, 2, False, False),
)

NA_QROWS = 8
NA_KROWS = 16
NA_TQ = NA_QROWS * GRID_W
NA_TK = NA_KROWS * GRID_W

VMEM_LIMIT_BYTES = 56 * 1024 * 1024


def _params(sem):
    return pltpu.CompilerParams(dimension_semantics=sem, vmem_limit_bytes=VMEM_LIMIT_BYTES)


def _rms(xf, g):
    ms = jnp.mean(xf * xf, axis=-1, keepdims=True)
    return xf * lax.rsqrt(ms + EPS) * g


def _dot(a, b):
    return jnp.dot(a, b, preferred_element_type=F32)


def _dot_nt(a, b):
    return lax.dot_general(a, b, (((1,), (1,)), ((), ())), preferred_element_type=F32)


def _pick_tile(n, want):
    t = min(n, want)
    while n % t:
        t //= 2
    return t


def _ffn_kernel(x_ref, g_ref, wg_ref, wu_ref, wd_ref, o_ref, h_ref, acc_ref):
    j = pl.program_id(1)

    @pl.when(j == 0)
    def _():
        h_ref[...] = _rms(x_ref[...], g_ref[...]).astype(BF16)

    h = h_ref[...]
    a = _dot(h, wg_ref[...])
    u = _dot(h, wu_ref[...])
    act = (a * jax.nn.sigmoid(a) * u).astype(BF16)
    part = _dot(act, wd_ref[...])

    @pl.when(j == 0)
    def _():
        acc_ref[...] = part

    @pl.when(j > 0)
    def _():
        acc_ref[...] += part

    @pl.when(j == pl.num_programs(1) - 1)
    def _():
        o_ref[...] = x_ref[...] + 0.5 * acc_ref[...]


def _ffn(x, g, wg, wu, wd):
    t, d = x.shape
    f = wg.shape[1]
    tm = _pick_tile(t, 512)
    tf = _pick_tile(f, 512)
    return pl.pallas_call(
        _ffn_kernel,
        out_shape=jax.ShapeDtypeStruct((t, d), F32),
        grid=(t // tm, f // tf),
        in_specs=[
            pl.BlockSpec((tm, d), lambda i, j: (i, 0)),
            pl.BlockSpec((1, d), lambda i, j: (0, 0)),
            pl.BlockSpec((d, tf), lambda i, j: (0, j)),
            pl.BlockSpec((d, tf), lambda i, j: (0, j)),
            pl.BlockSpec((tf, d), lambda i, j: (j, 0)),
        ],
        out_specs=pl.BlockSpec((tm, d), lambda i, j: (i, 0)),
        scratch_shapes=[pltpu.VMEM((tm, d), BF16), pltpu.VMEM((tm, d), F32)],
        compiler_params=_params(("parallel", "arbitrary")),
        name="ffn",
    )(x, g, wg, wu, wd)


def _swap_rotary_halves(y):
    lane = lax.broadcasted_iota(jnp.int32, y.shape, 1)
    first = (lane & 32) == 0
    return jnp.where(first, pltpu.roll(y, 96, 1), pltpu.roll(y, 32, 1))


def _qkv_kernel(x_ref, g_ref, w_ref, gain_ref, cos_ref, sin_ref, o_ref):
    h = _rms(x_ref[...], g_ref[...]).astype(BF16)
    for h0, nh, norm, rope in QKV_SEGMENTS:
        c0 = h0 * HEAD_DIM
        y = _dot(h, w_ref[:, c0:c0 + nh * HEAD_DIM])
        for hh in range(nh):
            lo = hh * HEAD_DIM
            yh = y[:, lo:lo + HEAD_DIM]
            if norm:
                yh = _rms(yh, gain_ref[:, c0 + lo:c0 + lo + HEAD_DIM])
            if rope:
                yh = yh * cos_ref[...] + _swap_rotary_halves(yh) * sin_ref[...]
            o_ref[:, c0 + lo:c0 + lo + HEAD_DIM] = yh.astype(BF16)


def _qkv(x, g, w_in, gain, cos, sin, seq):
    t, d = x.shape
    tm = _pick_tile(seq, 256)
    nseq = seq // tm
    return pl.pallas_call(
        _qkv_kernel,
        out_shape=jax.ShapeDtypeStruct((t, D_IN), BF16),
        grid=(t // tm,),
        in_specs=[
            pl.BlockSpec((tm, d), lambda i: (i, 0)),
            pl.BlockSpec((1, d), lambda i: (0, 0)),
            pl.BlockSpec((d, D_IN), lambda i: (0, 0), pipeline_mode=pl.Buffered(1)),
            pl.BlockSpec((1, D_IN), lambda i: (0, 0)),
            pl.BlockSpec((tm, HEAD_DIM), lambda i: (i % nseq, 0)),
            pl.BlockSpec((tm, HEAD_DIM), lambda i: (i % nseq, 0)),
        ],
        out_specs=pl.BlockSpec((tm, D_IN), lambda i: (i, 0)),
        compiler_params=_params(("parallel",)),
        name="qkv",
    )(x, g, w_in, gain, cos, sin)


def _na_kernel(q_ref, k_ref, v_ref, bias_ref, o_ref, *, rows):
    i = pl.program_id(2)
    start_row = jnp.clip(i * NA_QROWS - NA_ROWS // 2, 0, rows - NA_KROWS)
    start = pl.multiple_of(start_row * GRID_W, GRID_W)
    kw = k_ref[pl.ds(start, NA_TK), :]
    vw = v_ref[pl.ds(start, NA_TK), :]
    s = _dot_nt(q_ref[...], kw) + bias_ref[...]
    m = jnp.max(s, axis=-1, keepdims=True)
    p = jnp.exp(s - m)
    l = jnp.sum(p, axis=-1, keepdims=True)
    o = _dot(p.astype(BF16), vw)
    o_ref[...] = (o / l).astype(BF16)


def _na(qkv, bias, batch, seq):
    rows = seq // GRID_W
    assert seq % NA_TQ == 0 and rows >= NA_KROWS
    nblk = seq // NA_TQ

    def variant(i):
        return jnp.where(i == 0, 0, jnp.where(i == nblk - 1, 2, 1))

    return pl.pallas_call(
        functools.partial(_na_kernel, rows=rows),
        out_shape=jax.ShapeDtypeStruct((batch, seq, NA_HEADS * HEAD_DIM), BF16),
        grid=(NA_HEADS, batch, nblk),
        in_specs=[
            pl.BlockSpec((None, NA_TQ, HEAD_DIM), lambda h, b, i: (b, i, NA_Q0 + h)),
            pl.BlockSpec((None, seq, HEAD_DIM), lambda h, b, i: (b, 0, NA_K0 + h)),
            pl.BlockSpec((None, seq, HEAD_DIM), lambda h, b, i: (b, 0, NA_V0 + h)),
            pl.BlockSpec((None, None, NA_TQ, NA_TK), lambda h, b, i: (h, variant(i), 0, 0)),
        ],
        out_specs=pl.BlockSpec((None, NA_TQ, HEAD_DIM), lambda h, b, i: (b, i, h)),
        compiler_params=_params(("parallel", "parallel", "arbitrary")),
        name="na",
    )(qkv, qkv, qkv, bias)


def _na_bias_table(rel_bias):
    rel = rel_bias.astype(F32)
    cq = np.arange(GRID_W)
    kc = np.arange(GRID_W)
    cs = np.clip(cq - NA_COLS // 2, 0, GRID_W - NA_COLS)
    col_ok = (kc[None, :] >= cs[:, None]) & (kc[None, :] < cs[:, None] + NA_COLS)
    col_idx = np.where(col_ok, kc[None, :] - cq[:, None] + NA_COLS - 1, 0)
    t1 = jnp.where(col_ok[None, None], rel[:, :, col_idx], MASKED)

    rows = 3 * NA_QROWS
    qr = np.arange(NA_QROWS)
    kr = np.arange(NA_KROWS)
    row_idx = np.zeros((3, NA_QROWS, NA_KROWS), np.int32)
    row_ok = np.zeros((3, NA_QROWS, NA_KROWS), bool)
    for v in range(3):
        r = v * NA_QROWS + qr
        start_row = int(np.clip(v * NA_QROWS - NA_ROWS // 2, 0, rows - NA_KROWS))
        key_row = start_row + kr
        rs = np.clip(r - NA_ROWS // 2, 0, rows - NA_ROWS)
        ok = (key_row[None, :] >= rs[:, None]) & (key_row[None, :] < rs[:, None] + NA_ROWS)
        row_ok[v] = ok
        row_idx[v] = np.where(ok, key_row[None, :] - r[:, None] + NA_ROWS - 1, 0)
    t2 = jnp.take(t1, jnp.asarray(row_idx.reshape(-1)), axis=1)
    t2 = t2.reshape(NA_HEADS, 3, NA_QROWS, NA_KROWS, GRID_W, GRID_W)
    t2 = jnp.where(row_ok[None, :, :, :, None, None], t2, MASKED)
    t2 = t2.transpose(0, 1, 2, 4, 3, 5)
    return t2.reshape(NA_HEADS, 3, NA_TQ, NA_TK)


def _ga_kernel(q_ref, k_ref, v_ref, o_ref, qs_ref, m_ref, l_ref, acc_ref, *, tq, tk, group):
    for g in range(group):
        qs_ref[g * tq:(g + 1) * tq, :] = q_ref[:, g * HEAD_DIM:(g + 1) * HEAD_DIM]
    m_ref[...] = jnp.full(m_ref.shape, MASKED, F32)
    l_ref[...] = jnp.zeros(l_ref.shape, F32)
    acc_ref[...] = jnp.zeros(acc_ref.shape, F32)

    def body(c, carry):
        start = pl.multiple_of(c * tk, tk)
        kc = k_ref[pl.ds(start, tk), :]
        vc = v_ref[pl.ds(start, tk), :]
        s = _dot_nt(qs_ref[...], kc)
        m_prev = m_ref[...]
        m_new = jnp.maximum(m_prev, jnp.max(s, axis=-1, keepdims=True))
        alpha = jnp.exp(m_prev - m_new)
        p = jnp.exp(s - m_new)
        l_ref[...] = alpha * l_ref[...] + jnp.sum(p, axis=-1, keepdims=True)
        acc_ref[...] = alpha * acc_ref[...] + _dot(p.astype(BF16), vc)
        m_ref[...] = m_new
        return carry

    lax.fori_loop(0, k_ref.shape[0] // tk, body, 0)
    o = acc_ref[...] / l_ref[...]
    for g in range(group):
        o_ref[:, g * HEAD_DIM:(g + 1) * HEAD_DIM] = o[g * tq:(g + 1) * tq, :].astype(BF16)


def _ga(qkv, batch, seq):
    group = GA_HEADS // GA_KV_HEADS
    tq = _pick_tile(seq, 256)
    tk = _pick_tile(seq, 512)
    gw = group * HEAD_DIM
    return pl.pallas_call(
        functools.partial(_ga_kernel, tq=tq, tk=tk, group=group),
        out_shape=jax.ShapeDtypeStruct((batch, seq, GA_HEADS * HEAD_DIM), BF16),
        grid=(batch, GA_KV_HEADS, seq // tq),
        in_specs=[
            pl.BlockSpec((None, tq, gw), lambda b, kv, i: (b, i, GA_Q0 // group + kv)),
            pl.BlockSpec((None, seq, HEAD_DIM), lambda b, kv, i: (b, 0, GA_K0 + kv)),
            pl.BlockSpec((None, seq, HEAD_DIM), lambda b, kv, i: (b, 0, GA_V0 + kv)),
        ],
        out_specs=pl.BlockSpec((None, tq, gw), lambda b, kv, i: (b, i, kv)),
        scratch_shapes=[
            pltpu.VMEM((group * tq, HEAD_DIM), BF16),
            pltpu.VMEM((group * tq, 1), F32),
            pltpu.VMEM((group * tq, 1), F32),
            pltpu.VMEM((group * tq, HEAD_DIM), F32),
        ],
        compiler_params=_params(("parallel", "parallel", "arbitrary")),
        name="ga",
    )(qkv, qkv, qkv)


def _sw_kernel(q_ref, k_ref, v_ref, slope_ref, sink_ref, o_ref, *, seq, nq, group):
    span = Q_BLOCK + 2 * WINDOW
    step = pl.program_id(2)
    slope = slope_ref[...]
    sink = sink_ref[...]
    for n in range(nq):
        start = (step * nq + n) * Q_BLOCK
        ws = pl.multiple_of(jnp.clip(start - WINDOW, 0, seq - span), Q_BLOCK)
        kw = k_ref[pl.ds(ws, span), :]
        vw = v_ref[pl.ds(ws, span), :]
        q = jnp.concatenate(
            [q_ref[n * Q_BLOCK:(n + 1) * Q_BLOCK, g * HEAD_DIM:(g + 1) * HEAD_DIM] for g in range(group)],
            axis=0)
        s = _dot_nt(q, kw).reshape(group, Q_BLOCK, span)
        qpos = start + lax.broadcasted_iota(jnp.int32, (Q_BLOCK, span), 0)
        kpos = ws + lax.broadcasted_iota(jnp.int32, (Q_BLOCK, span), 1)
        dist = jnp.abs(qpos - kpos)
        s = s - slope * dist.astype(F32)[None]
        s = jnp.where((dist <= WINDOW)[None], s, MASKED)
        m = jnp.maximum(jnp.max(s, axis=-1, keepdims=True), sink)
        p = jnp.exp(s - m)
        denom = jnp.sum(p, axis=-1, keepdims=True) + jnp.exp(sink - m)
        o = _dot(p.reshape(group * Q_BLOCK, span).astype(BF16), vw)
        o = o / denom.reshape(group * Q_BLOCK, 1)
        for g in range(group):
            o_ref[n * Q_BLOCK:(n + 1) * Q_BLOCK, g * HEAD_DIM:(g + 1) * HEAD_DIM] = (
                o[g * Q_BLOCK:(g + 1) * Q_BLOCK, :].astype(BF16))


def _sw(qkv, slopes, sink, batch, seq):
    group = SW_HEADS // SW_KV_HEADS
    nq = 4
    tq = nq * Q_BLOCK
    assert seq % tq == 0 and seq >= Q_BLOCK + 2 * WINDOW
    gw = group * HEAD_DIM
    return pl.pallas_call(
        functools.partial(_sw_kernel, seq=seq, nq=nq, group=group),
        out_shape=jax.ShapeDtypeStruct((batch, seq, SW_HEADS * HEAD_DIM), BF16),
        grid=(batch, SW_KV_HEADS, seq // tq),
        in_specs=[
            pl.BlockSpec((None, tq, gw), lambda b, kv, i: (b, i, SW_Q0 // group + kv)),
            pl.BlockSpec((None, seq, HEAD_DIM), lambda b, kv, i: (b, 0, SW_K0 + kv)),
            pl.BlockSpec((None, seq, HEAD_DIM), lambda b, kv, i: (b, 0, SW_V0 + kv)),
            pl.BlockSpec((None, group, 1, 1), lambda b, kv, i: (kv, 0, 0, 0)),
            pl.BlockSpec((None, group, 1, 1), lambda b, kv, i: (kv, 0, 0, 0)),
        ],
        out_specs=pl.BlockSpec((None, tq, gw), lambda b, kv, i: (b, i, kv)),
        compiler_params=_params(("parallel", "parallel", "arbitrary")),
        name="sw",
    )(qkv, qkv, qkv, slopes, sink)


def _merge_kernel(x_ref, g_ref, oa_ref, ob_ref, oc_ref, wg_ref, wb_ref, wo_ref, o_ref, h_ref, acc_ref):
    j = pl.program_id(1)

    @pl.when(j == 0)
    def _():
        h_ref[...] = _rms(x_ref[...], g_ref[...]).astype(BF16)

    h = h_ref[...]
    merged = None
    for n, br_ref in enumerate((oa_ref, ob_ref, oc_ref)):
        term = jax.nn.sigmoid(_dot(h, wg_ref[n])) * _dot(br_ref[...], wb_ref[n])
        merged = term if merged is None else merged + term
    part = _dot(merged.astype(BF16), wo_ref[...])

    @pl.when(j == 0)
    def _():
        acc_ref[...] = part

    @pl.when(j > 0)
    def _():
        acc_ref[...] += part

    @pl.when(j == pl.num_programs(1) - 1)
    def _():
        o_ref[...] = x_ref[...] + acc_ref[...]


def _merge(x, g, oa, ob, oc, w_gate, w_branch, w_out):
    t, d = x.shape
    bw = oa.shape[1]
    tm = _pick_tile(t, 512)
    tn = _pick_tile(d, 256)
    return pl.pallas_call(
        _merge_kernel,
        out_shape=jax.ShapeDtypeStruct((t, d), F32),
        grid=(t // tm, d // tn),
        in_specs=[
            pl.BlockSpec((tm, d), lambda i, j: (i, 0)),
            pl.BlockSpec((1, d), lambda i, j: (0, 0)),
            pl.BlockSpec((tm, bw), lambda i, j: (i, 0)),
            pl.BlockSpec((tm, bw), lambda i, j: (i, 0)),
            pl.BlockSpec((tm, bw), lambda i, j: (i, 0)),
            pl.BlockSpec((3, d, tn), lambda i, j: (0, 0, j)),
            pl.BlockSpec((3, bw, tn), lambda i, j: (0, 0, j)),
            pl.BlockSpec((tn, d), lambda i, j: (j, 0)),
        ],
        out_specs=pl.BlockSpec((tm, d), lambda i, j: (i, 0)),
        scratch_shapes=[pltpu.VMEM((tm, d), BF16), pltpu.VMEM((tm, d), F32)],
        compiler_params=_params(("parallel", "arbitrary")),
        name="merge",
    )(x, g, oa, ob, oc, w_gate, w_branch, w_out)


def _rope_tables(seq):
    quarter = HEAD_DIM // 4
    freqs = ROPE_THETA ** (-jnp.arange(quarter, dtype=F32) / quarter)
    t = jnp.arange(seq)
    row = (t // GRID_W).astype(F32)
    col = (t % GRID_W).astype(F32)
    ang_r = row[:, None] * freqs[None, :]
    ang_c = col[:, None] * freqs[None, :]
    cos = jnp.concatenate([jnp.cos(ang_r)] * 2 + [jnp.cos(ang_c)] * 2, axis=-1)
    sin = jnp.concatenate([-jnp.sin(ang_r), jnp.sin(ang_r), -jnp.sin(ang_c), jnp.sin(ang_c)], axis=-1)
    return cos, sin


def _gain_row(na_q, na_k, ga_q, ga_k, sw_q, sw_k):
    scale = HEAD_DIM ** -0.5
    one = jnp.ones((HEAD_DIM,), F32)
    parts = ([na_q * scale] * 8 + [na_k] * 8 + [one] * 8 + [ga_q * scale] * 8 + [ga_k] * 2 + [one] * 2
             + [sw_q * scale] * 8 + [sw_k] * 2 + [one] * 2)
    return jnp.concatenate([p.astype(F32) for p in parts]).reshape(1, D_IN)


def _trunk(x3, layers, rope):
    batch, seq, d = x3.shape
    x = x3.reshape(batch * seq, d)
    cos, sin = rope
    for p in layers:
        x = _ffn(x, p["ffn1_norm"], p["ffn1_wg"], p["ffn1_wu"], p["ffn1_wd"])
        qkv = _qkv(x, p["mix_norm"], p["w_in"], p["gain"], cos, sin, seq).reshape(batch, seq, D_IN)
        oa = _na(qkv, p["na_bias"], batch, seq).reshape(batch * seq, -1)
        ob = _ga(qkv, batch, seq).reshape(batch * seq, -1)
        oc = _sw(qkv, p["sw_slopes"], p["sw_sink"], batch, seq).reshape(batch * seq, -1)
        x = _merge(x, p["mix_norm"], oa, ob, oc, p["w_gate"], p["w_branch"], p["w_out"])
        x = _ffn(x, p["ffn2_norm"], p["ffn2_wg"], p["ffn2_wu"], p["ffn2_wd"])
    return x.reshape(batch, seq, d)


def kernel(x_prompt, x_sample, ffn1_norm, ffn1_wg, ffn1_wu, ffn1_wd, mix_norm, w_in, w_gate, na_q_norm, na_k_norm, na_rel_bias, ga_q_norm, ga_k_norm, sw_q_norm, sw_k_norm, sw_sink, w_branch, w_out, ffn2_norm, ffn2_wg, ffn2_wu, ffn2_wd):
    depth = w_in.shape[0]
    d = x_prompt.shape[-1]
    sw_group = SW_HEADS // SW_KV_HEADS
    slopes = 2.0 ** (-8.0 * jnp.arange(1, SW_HEADS + 1, dtype=F32) / SW_HEADS)
    slopes = slopes.reshape(SW_KV_HEADS, sw_group, 1, 1)
    layers = []
    for l in range(depth):
        layers.append(dict(
            ffn1_norm=ffn1_norm[l].reshape(1, d).astype(F32),
            ffn1_wg=ffn1_wg[l].astype(BF16), ffn1_wu=ffn1_wu[l].astype(BF16), ffn1_wd=ffn1_wd[l].astype(BF16),
            mix_norm=mix_norm[l].reshape(1, d).astype(F32),
            w_in=w_in[l].astype(BF16),
            gain=_gain_row(na_q_norm[l], na_k_norm[l], ga_q_norm[l], ga_k_norm[l], sw_q_norm[l], sw_k_norm[l]),
            na_bias=_na_bias_table(na_rel_bias[l]),
            sw_slopes=slopes,
            sw_sink=sw_sink[l].astype(F32).reshape(SW_KV_HEADS, sw_group, 1, 1),
            w_gate=w_gate[l].astype(BF16), w_branch=w_branch[l].astype(BF16), w_out=w_out[l].astype(BF16),
            ffn2_norm=ffn2_norm[l].reshape(1, d).astype(F32),
            ffn2_wg=ffn2_wg[l].astype(BF16), ffn2_wu=ffn2_wu[l].astype(BF16), ffn2_wd=ffn2_wd[l].astype(BF16),
        ))
    outs = []
    for x3 in (x_prompt, x_sample):
        outs.append(_trunk(x3, layers, _rope_tables(x3.shape[1])))
    return tuple(outs)
```

```python
import functools
import math

import numpy as np
import jax
import jax.numpy as jnp
from jax import lax
from jax.experimental import pallas as pl
from jax.experimental.pallas import tpu as pltpu

F32 = jnp.float32
BF16 = jnp.bfloat16

EPS = 1e-6
HEAD_DIM = 128
GRID_W = 64
NA_HEADS = 8
NA_ROWS = 8
NA_COLS = 16
GA_HEADS = 8
GA_KV_HEADS = 2
SW_HEADS = 8
SW_KV_HEADS = 2
WINDOW = 128
Q_BLOCK = 128
ROPE_THETA = 10000.0
MASKED = -1e30
LOG2E = math.log2(math.e)

NA_Q0, NA_K0, NA_V0 = 0, 8, 16
GA_Q0, GA_K0, GA_V0 = 24, 32, 34
SW_Q0, SW_K0, SW_V0 = 36, 44, 46
N_IN_HEADS = 48
D_IN = N_IN_HEADS * HEAD_DIM

QKV_SEGMENTS = (
    (NA_Q0, 8, True, False), (NA_K0, 8, True, False), (NA_V0, 8, False, False),
    (GA_Q0, 8, True, True), (GA_K0, 2, True, True), (GA_V0, 2, False, False),
    (SW_Q0, 8, True, False), (SW_K0, 2, True, False), (SW_V0, 2, False, False),
)

NA_QROWS = 8
NA_KROWS = 16
NA_TQ = NA_QROWS * GRID_W
NA_TK = NA_KROWS * GRID_W

VMEM_LIMIT_BYTES = 56 * 1024 * 1024


def _params(sem):
    return pltpu.CompilerParams(dimension_semantics=sem, vmem_limit_bytes=VMEM_LIMIT_BYTES)


def _rms(xf, g):
    ms = jnp.mean(xf * xf, axis=-1, keepdims=True)
    return xf * lax.rsqrt(ms + EPS) * g


def _dot(a, b):
    return jnp.dot(a, b, preferred_element_type=F32)


def _dot_nt(a, b):
    return lax.dot_general(a, b, (((1,), (1,)), ((), ())), preferred_element_type=F32)


def _pick_tile(n, want):
    t = min(n, want)
    while n % t:
        t //= 2
    return t


def _ffn_kernel(x_ref, g_ref, wg_ref, wu_ref, wd_ref, o_ref, h_ref, acc_ref):
    j = pl.program_id(1)

    @pl.when(j == 0)
    def _():
        h_ref[...] = _rms(x_ref[...], g_ref[...]).astype(BF16)
        acc_ref[...] = jnp.zeros(acc_ref.shape, F32)

    h = h_ref[...]
    a = _dot(h, wg_ref[...])
    u = _dot(h, wu_ref[...])
    act = (a * jax.nn.sigmoid(a) * u).astype(BF16)
    acc_ref[...] += _dot(act, wd_ref[...])

    @pl.when(j == pl.num_programs(1) - 1)
    def _():
        o_ref[...] = x_ref[...] + 0.5 * acc_ref[...]


def _ffn(x, g, wg, wu, wd):
    t, d = x.shape
    f = wg.shape[1]
    tm = _pick_tile(t, 512)
    tf = _pick_tile(f, 512)
    return pl.pallas_call(
        _ffn_kernel,
        out_shape=jax.ShapeDtypeStruct((t, d), F32),
        grid=(t // tm, f // tf),
        in_specs=[
            pl.BlockSpec((tm, d), lambda i, j: (i, 0)),
            pl.BlockSpec((1, d), lambda i, j: (0, 0)),
            pl.BlockSpec((d, tf), lambda i, j: (0, j)),
            pl.BlockSpec((d, tf), lambda i, j: (0, j)),
            pl.BlockSpec((tf, d), lambda i, j: (j, 0)),
        ],
        out_specs=pl.BlockSpec((tm, d), lambda i, j: (i, 0)),
        scratch_shapes=[pltpu.VMEM((tm, d), BF16), pltpu.VMEM((tm, d), F32)],
        compiler_params=_params(("parallel", "arbitrary")),
        name="ffn",
    )(x, g, wg, wu, wd)


def _swap_rotary_halves(y):
    lane = lax.broadcasted_iota(jnp.int32, y.shape, 1)
    first = (lane & 32) == 0
    return jnp.where(first, pltpu.roll(y, 96, 1), pltpu.roll(y, 32, 1))


def _qkv_kernel(x_ref, g_ref, w_ref, gain_ref, cos_ref, sin_ref, o_ref):
    h = _rms(x_ref[...], g_ref[...]).astype(BF16)
    for h0, nh, norm, rope in QKV_SEGMENTS:
        c0 = h0 * HEAD_DIM
        y = _dot(h, w_ref[:, c0:c0 + nh * HEAD_DIM])
        for hh in range(nh):
            lo = hh * HEAD_DIM
            yh = y[:, lo:lo + HEAD_DIM]
            if norm:
                yh = _rms(yh, gain_ref[:, c0 + lo:c0 + lo + HEAD_DIM])
            if rope:
                yh = yh * cos_ref[...] + _swap_rotary_halves(yh) * sin_ref[...]
            o_ref[:, c0 + lo:c0 + lo + HEAD_DIM] = yh.astype(BF16)


def _qkv(x, g, w_in, gain, cos, sin, seq):
    t, d = x.shape
    tm = _pick_tile(seq, 256)
    nseq = seq // tm
    return pl.pallas_call(
        _qkv_kernel,
        out_shape=jax.ShapeDtypeStruct((t, D_IN), BF16),
        grid=(t // tm,),
        in_specs=[
            pl.BlockSpec((tm, d), lambda i: (i, 0)),
            pl.BlockSpec((1, d), lambda i: (0, 0)),
            pl.BlockSpec((d, D_IN), lambda i: (0, 0), pipeline_mode=pl.Buffered(1)),
            pl.BlockSpec((1, D_IN), lambda i: (0, 0)),
            pl.BlockSpec((tm, HEAD_DIM), lambda i: (i % nseq, 0)),
            pl.BlockSpec((tm, HEAD_DIM), lambda i: (i % nseq, 0)),
        ],
        out_specs=pl.BlockSpec((tm, D_IN), lambda i: (i, 0)),
        compiler_params=_params(("parallel",)),
        name="qkv",
    )(x, g, w_in, gain, cos, sin)


def _edge_variant(i, nblk):
    return jnp.where(i == 0, 0, jnp.where(i == nblk - 1, 2, 1))


def _two_stage_loop(nblk, logits, finish, sa_ref, sb_ref):
    logits(0, sa_ref)

    def body(j, carry):
        i = 2 * j
        logits(i + 1, sb_ref)
        finish(i, sa_ref)
        logits(i + 2, sa_ref)
        finish(i + 1, sb_ref)
        return carry

    lax.fori_loop(0, nblk // 2 - 1, body, 0)
    logits(nblk - 1, sb_ref)
    finish(nblk - 2, sa_ref)
    finish(nblk - 1, sb_ref)


def _na_kernel(q_ref, k_ref, v_ref, bias_ref, o_ref, v1_ref, sa_ref, sb_ref, *, rows):
    nblk = q_ref.shape[0] // NA_TQ
    v1_ref[:, :HEAD_DIM] = v_ref[...]
    v1_ref[:, HEAD_DIM:] = jnp.ones(v_ref.shape, BF16)

    def window(i):
        start_row = jnp.clip(i * NA_QROWS - NA_ROWS // 2, 0, rows - NA_KROWS)
        return pl.multiple_of(start_row * GRID_W, GRID_W)

    def logits(i, dst_ref):
        q = q_ref[pl.ds(pl.multiple_of(i * NA_TQ, NA_TQ), NA_TQ), :]
        dst_ref[...] = _dot_nt(q, k_ref[pl.ds(window(i), NA_TK), :])

    def finish(i, src_ref):
        s = src_ref[...] + bias_ref[_edge_variant(i, nblk)]
        m = jnp.max(s, axis=-1, keepdims=True)
        p = jnp.exp2(s - m)
        pv = _dot(p.astype(BF16), v1_ref[pl.ds(window(i), NA_TK), :])
        o = pv[:, :HEAD_DIM] / pv[:, HEAD_DIM:]
        o_ref[pl.ds(pl.multiple_of(i * NA_TQ, NA_TQ), NA_TQ), :] = o.astype(BF16)

    _two_stage_loop(nblk, logits, finish, sa_ref, sb_ref)


def _na(qkv, bias, batch, seq):
    rows = seq // GRID_W
    assert seq % (2 * NA_TQ) == 0 and rows >= NA_KROWS
    return pl.pallas_call(
        functools.partial(_na_kernel, rows=rows),
        out_shape=jax.ShapeDtypeStruct((batch, seq, NA_HEADS * HEAD_DIM), BF16),
        grid=(NA_HEADS, batch),
        in_specs=[
            pl.BlockSpec((None, seq, HEAD_DIM), lambda h, b: (b, 0, NA_Q0 + h)),
            pl.BlockSpec((None, seq, HEAD_DIM), lambda h, b: (b, 0, NA_K0 + h)),
            pl.BlockSpec((None, seq, HEAD_DIM), lambda h, b: (b, 0, NA_V0 + h)),
            pl.BlockSpec((None, 3, NA_TQ, NA_TK), lambda h, b: (h, 0, 0, 0)),
        ],
        out_specs=pl.BlockSpec((None, seq, HEAD_DIM), lambda h, b: (b, 0, h)),
        scratch_shapes=[
            pltpu.VMEM((seq, 2 * HEAD_DIM), BF16),
            pltpu.VMEM((NA_TQ, NA_TK), F32),
            pltpu.VMEM((NA_TQ, NA_TK), F32),
        ],
        compiler_params=_params(("parallel", "parallel")),
        name="na",
    )(qkv, qkv, qkv, bias)


def _na_bias_table(rel_bias):
    rel = rel_bias.astype(F32) * LOG2E
    cq = np.arange(GRID_W)
    kc = np.arange(GRID_W)
    cs = np.clip(cq - NA_COLS // 2, 0, GRID_W - NA_COLS)
    col_ok = (kc[None, :] >= cs[:, None]) & (kc[None, :] < cs[:, None] + NA_COLS)
    col_idx = np.where(col_ok, kc[None, :] - cq[:, None] + NA_COLS - 1, 0)
    t1 = jnp.where(col_ok[None, None], rel[:, :, col_idx], MASKED)

    rows = 3 * NA_QROWS
    qr = np.arange(NA_QROWS)
    kr = np.arange(NA_KROWS)
    row_idx = np.zeros((3, NA_QROWS, NA_KROWS), np.int32)
    row_ok = np.zeros((3, NA_QROWS, NA_KROWS), bool)
    for v in range(3):
        r = v * NA_QROWS + qr
        start_row = int(np.clip(v * NA_QROWS - NA_ROWS // 2, 0, rows - NA_KROWS))
        key_row = start_row + kr
        rs = np.clip(r - NA_ROWS // 2, 0, rows - NA_ROWS)
        ok = (key_row[None, :] >= rs[:, None]) & (key_row[None, :] < rs[:, None] + NA_ROWS)
        row_ok[v] = ok
        row_idx[v] = np.where(ok, key_row[None, :] - r[:, None] + NA_ROWS - 1, 0)
    t2 = jnp.take(t1, jnp.asarray(row_idx.reshape(-1)), axis=1)
    t2 = t2.reshape(NA_HEADS, 3, NA_QROWS, NA_KROWS, GRID_W, GRID_W)
    t2 = jnp.where(row_ok[None, :, :, :, None, None], t2, MASKED)
    t2 = t2.transpose(0, 1, 2, 4, 3, 5)
    return t2.reshape(NA_HEADS, 3, NA_TQ, NA_TK)


def _ga_kernel(q_ref, k_ref, v_ref, o_ref, qs_ref, v1_ref, m_ref, acc_ref, sa_ref, sb_ref, *, tq, tk, group):
    @pl.when(pl.program_id(2) == 0)
    def _():
        v1_ref[:, :HEAD_DIM] = v_ref[...]
        v1_ref[:, HEAD_DIM:] = jnp.ones(v_ref.shape, BF16)

    for g in range(group):
        qs_ref[g * tq:(g + 1) * tq, :] = q_ref[:, g * HEAD_DIM:(g + 1) * HEAD_DIM]
    m_ref[...] = jnp.full(m_ref.shape, MASKED, F32)
    acc_ref[...] = jnp.zeros(acc_ref.shape, F32)
    lane_tiles = tk // HEAD_DIM
    n_chunks = k_ref.shape[0] // tk

    def logits(c, dst_ref):
        start = pl.multiple_of(c * tk, tk)
        dst_ref[...] = _dot_nt(qs_ref[...], k_ref[pl.ds(start, tk), :])

    def accumulate(c, src_ref):
        start = pl.multiple_of(c * tk, tk)
        s = src_ref[...]
        m_prev = m_ref[...]
        m_new = jnp.maximum(m_prev, jnp.max(s, axis=-1, keepdims=True))
        alpha = jnp.exp2(m_prev - m_new)
        p = jnp.exp2(s - jnp.concatenate([m_new] * lane_tiles, axis=1))
        pv = _dot(p.astype(BF16), v1_ref[pl.ds(start, tk), :])
        acc_ref[...] = jnp.concatenate([alpha, alpha], axis=1) * acc_ref[...] + pv
        m_ref[...] = m_new

    _two_stage_loop(n_chunks, logits, accumulate, sa_ref, sb_ref)
    o = acc_ref[:, :HEAD_DIM] / acc_ref[:, HEAD_DIM:]
    for g in range(group):
        o_ref[:, g * HEAD_DIM:(g + 1) * HEAD_DIM] = o[g * tq:(g + 1) * tq, :].astype(BF16)


def _ga(qkv, batch, seq):
    group = GA_HEADS // GA_KV_HEADS
    tq = _pick_tile(seq, 256)
    tk = _pick_tile(seq, 512)
    assert (seq // tk) % 2 == 0
    gw = group * HEAD_DIM
    return pl.pallas_call(
        functools.partial(_ga_kernel, tq=tq, tk=tk, group=group),
        out_shape=jax.ShapeDtypeStruct((batch, seq, GA_HEADS * HEAD_DIM), BF16),
        grid=(batch, GA_KV_HEADS, seq // tq),
        in_specs=[
            pl.BlockSpec((None, tq, gw), lambda b, kv, i: (b, i, GA_Q0 // group + kv)),
            pl.BlockSpec((None, seq, HEAD_DIM), lambda b, kv, i: (b, 0, GA_K0 + kv)),
            pl.BlockSpec((None, seq, HEAD_DIM), lambda b, kv, i: (b, 0, GA_V0 + kv)),
        ],
        out_specs=pl.BlockSpec((None, tq, gw), lambda b, kv, i: (b, i, kv)),
        scratch_shapes=[
            pltpu.VMEM((group * tq, HEAD_DIM), BF16),
            pltpu.VMEM((seq, 2 * HEAD_DIM), BF16),
            pltpu.VMEM((group * tq, HEAD_DIM), F32),
            pltpu.VMEM((group * tq, 2 * HEAD_DIM), F32),
            pltpu.VMEM((group * tq, tk), F32),
            pltpu.VMEM((group * tq, tk), F32),
        ],
        compiler_params=_params(("parallel", "parallel", "arbitrary")),
        name="ga",
    )(qkv, qkv, qkv)


def _sw_kernel(q_ref, k_ref, v_ref, bias_ref, sink_ref, o_ref, v1_ref, sa_ref, sb_ref, *, seq, group):
    span = Q_BLOCK + 2 * WINDOW
    nblk_seq = seq // Q_BLOCK
    nblk = q_ref.shape[0] // Q_BLOCK
    first = pl.program_id(2) * nblk

    @pl.when(pl.program_id(2) == 0)
    def _():
        v1_ref[:, :HEAD_DIM] = v_ref[...]
        v1_ref[:, HEAD_DIM:] = jnp.ones(v_ref.shape, BF16)

    def window(n):
        return pl.multiple_of(jnp.clip((first + n) * Q_BLOCK - WINDOW, 0, seq - span), Q_BLOCK)

    def rows(n):
        return pl.ds(pl.multiple_of(n * Q_BLOCK, Q_BLOCK), Q_BLOCK)

    def logits(n, dst_ref):
        qb = q_ref[rows(n), :]
        q = jnp.concatenate([qb[:, g * HEAD_DIM:(g + 1) * HEAD_DIM] for g in range(group)], axis=0)
        dst_ref[...] = _dot_nt(q, k_ref[pl.ds(window(n), span), :])

    def finish(n, src_ref):
        s = src_ref[...] + bias_ref[_edge_variant(first + n, nblk_seq)]
        sink = sink_ref[...]
        m = jnp.maximum(jnp.max(s, axis=-1, keepdims=True), sink)
        p = jnp.exp2(s - jnp.concatenate([m] * (span // HEAD_DIM), axis=1))
        pv = _dot(p.astype(BF16), v1_ref[pl.ds(window(n), span), :])
        o = pv[:, :HEAD_DIM] / (pv[:, HEAD_DIM:] + jnp.exp2(sink - m))
        for g in range(group):
            o_ref[rows(n), g * HEAD_DIM:(g + 1) * HEAD_DIM] = o[g * Q_BLOCK:(g + 1) * Q_BLOCK, :].astype(BF16)

    _two_stage_loop(nblk, logits, finish, sa_ref, sb_ref)


def _sw_bias_table():
    group = SW_HEADS // SW_KV_HEADS
    span = Q_BLOCK + 2 * WINDOW
    slopes = 2.0 ** (-8.0 * np.arange(1, SW_HEADS + 1, dtype=np.float64) / SW_HEADS)
    qi = np.arange(Q_BLOCK)[:, None]
    kj = np.arange(span)[None, :]
    table = np.empty((SW_KV_HEADS, 3, group, Q_BLOCK, span), np.float32)
    for v in range(3):
        dist = np.abs(qi - (kj - v * WINDOW))
        for h in range(SW_HEADS):
            table[h // group, v, h % group] = np.where(dist <= WINDOW, -LOG2E * slopes[h] * dist, MASKED)
    return jnp.asarray(table.reshape(SW_KV_HEADS, 3, group * Q_BLOCK, span))


def _sw(qkv, bias, sink, batch, seq):
    group = SW_HEADS // SW_KV_HEADS
    span = Q_BLOCK + 2 * WINDOW
    tq = _pick_tile(seq, 16 * Q_BLOCK)
    assert tq % (2 * Q_BLOCK) == 0 and seq >= span
    gw = group * HEAD_DIM
    rows = group * Q_BLOCK
    return pl.pallas_call(
        functools.partial(_sw_kernel, seq=seq, group=group),
        out_shape=jax.ShapeDtypeStruct((batch, seq, SW_HEADS * HEAD_DIM), BF16),
        grid=(batch, SW_KV_HEADS, seq // tq),
        in_specs=[
            pl.BlockSpec((None, tq, gw), lambda b, kv, i: (b, i, SW_Q0 // group + kv)),
            pl.BlockSpec((None, seq, HEAD_DIM), lambda b, kv, i: (b, 0, SW_K0 + kv)),
            pl.BlockSpec((None, seq, HEAD_DIM), lambda b, kv, i: (b, 0, SW_V0 + kv)),
            pl.BlockSpec((None, 3, rows, span), lambda b, kv, i: (kv, 0, 0, 0)),
            pl.BlockSpec((None, rows, HEAD_DIM), lambda b, kv, i: (kv, 0, 0)),
        ],
        out_specs=pl.BlockSpec((None, tq, gw), lambda b, kv, i: (b, i, kv)),
        scratch_shapes=[
            pltpu.VMEM((seq, 2 * HEAD_DIM), BF16),
            pltpu.VMEM((rows, span), F32),
            pltpu.VMEM((rows, span), F32),
        ],
        compiler_params=_params(("parallel", "parallel", "arbitrary")),
        name="sw",
    )(qkv, qkv, qkv, bias, sink)


def _merge_kernel(x_ref, g_ref, oa_ref, ob_ref, oc_ref, wg_ref, wb_ref, wo_ref, o_ref, h_ref, merged_ref):
    j = pl.program_id(1)

    @pl.when(j == 0)
    def _():
        h_ref[...] = _rms(x_ref[...], g_ref[...]).astype(BF16)

    h = h_ref[...]
    merged = None
    for n, br_ref in enumerate((oa_ref, ob_ref, oc_ref)):
        term = jax.nn.sigmoid(_dot(h, wg_ref[n])) * _dot(br_ref[...], wb_ref[n])
        merged = term if merged is None else merged + term
    merged_ref[j] = merged.astype(BF16)

    @pl.when(j == pl.num_programs(1) - 1)
    def _():
        merged_all = jnp.concatenate([merged_ref[c] for c in range(merged_ref.shape[0])], axis=1)
        o_ref[...] = x_ref[...] + _dot(merged_all, wo_ref[...])


def _merge(x, g, oa, ob, oc, w_gate, w_branch, w_out):
    t, d = x.shape
    bw = oa.shape[1]
    tm = _pick_tile(t, 512)
    tn = _pick_tile(d, 256)
    return pl.pallas_call(
        _merge_kernel,
        out_shape=jax.ShapeDtypeStruct((t, d), F32),
        grid=(t // tm, d // tn),
        in_specs=[
            pl.BlockSpec((tm, d), lambda i, j: (i, 0)),
            pl.BlockSpec((1, d), lambda i, j: (0, 0)),
            pl.BlockSpec((tm, bw), lambda i, j: (i, 0)),
            pl.BlockSpec((tm, bw), lambda i, j: (i, 0)),
            pl.BlockSpec((tm, bw), lambda i, j: (i, 0)),
            pl.BlockSpec((3, d, tn), lambda i, j: (0, 0, j)),
            pl.BlockSpec((3, bw, tn), lambda i, j: (0, 0, j)),
            pl.BlockSpec((d, d), lambda i, j: (0, 0), pipeline_mode=pl.Buffered(1)),
        ],
        out_specs=pl.BlockSpec((tm, d), lambda i, j: (i, 0)),
        scratch_shapes=[pltpu.VMEM((tm, d), BF16), pltpu.VMEM((d // tn, tm, tn), BF16)],
        compiler_params=_params(("parallel", "arbitrary")),
        name="merge",
    )(x, g, oa, ob, oc, w_gate, w_branch, w_out)


def _rope_tables(seq):
    quarter = HEAD_DIM // 4
    freqs = ROPE_THETA ** (-jnp.arange(quarter, dtype=F32) / quarter)
    t = jnp.arange(seq)
    row = (t // GRID_W).astype(F32)
    col = (t % GRID_W).astype(F32)
    ang_r = row[:, None] * freqs[None, :]
    ang_c = col[:, None] * freqs[None, :]
    cos = jnp.concatenate([jnp.cos(ang_r)] * 2 + [jnp.cos(ang_c)] * 2, axis=-1)
    sin = jnp.concatenate([-jnp.sin(ang_r), jnp.sin(ang_r), -jnp.sin(ang_c), jnp.sin(ang_c)], axis=-1)
    return cos, sin


def _gain_row(na_q, na_k, ga_q, ga_k, sw_q, sw_k):
    scale = HEAD_DIM ** -0.5 * LOG2E
    one = jnp.ones((HEAD_DIM,), F32)
    parts = ([na_q * scale] * 8 + [na_k] * 8 + [one] * 8 + [ga_q * scale] * 8 + [ga_k] * 2 + [one] * 2
             + [sw_q * scale] * 8 + [sw_k] * 2 + [one] * 2)
    return jnp.concatenate([p.astype(F32) for p in parts]).reshape(1, D_IN)


def _sw_sink_rows(sink, group):
    per_head = (sink.astype(F32) * LOG2E).reshape(SW_KV_HEADS, group, 1, 1)
    return jnp.broadcast_to(per_head, (SW_KV_HEADS, group, Q_BLOCK, HEAD_DIM)).reshape(
        SW_KV_HEADS, group * Q_BLOCK, HEAD_DIM)


def _trunk(x3, layers, rope):
    batch, seq, d = x3.shape
    x = x3.reshape(batch * seq, d)
    cos, sin = rope
    for p in layers:
        x = _ffn(x, p["ffn1_norm"], p["ffn1_wg"], p["ffn1_wu"], p["ffn1_wd"])
        qkv = _qkv(x, p["mix_norm"], p["w_in"], p["gain"], cos, sin, seq).reshape(batch, seq, D_IN)
        oa = _na(qkv, p["na_bias"], batch, seq).reshape(batch * seq, -1)
        ob = _ga(qkv, batch, seq).reshape(batch * seq, -1)
        oc = _sw(qkv, p["sw_bias"], p["sw_sink"], batch, seq).reshape(batch * seq, -1)
        x = _merge(x, p["mix_norm"], oa, ob, oc, p["w_gate"], p["w_branch"], p["w_out"])
        x = _ffn(x, p["ffn2_norm"], p["ffn2_wg"], p["ffn2_wu"], p["ffn2_wd"])
    return x.reshape(batch, seq, d)


def kernel(x_prompt, x_sample, ffn1_norm, ffn1_wg, ffn1_wu, ffn1_wd, mix_norm, w_in, w_gate, na_q_norm, na_k_norm, na_rel_bias, ga_q_norm, ga_k_norm, sw_q_norm, sw_k_norm, sw_sink, w_branch, w_out, ffn2_norm, ffn2_wg, ffn2_wu, ffn2_wd):
    depth = w_in.shape[0]
    d = x_prompt.shape[-1]
    sw_group = SW_HEADS // SW_KV_HEADS
    sw_bias = _sw_bias_table()
    layers = []
    for l in range(depth):
        layers.append(dict(
            ffn1_norm=ffn1_norm[l].reshape(1, d).astype(F32),
            ffn1_wg=ffn1_wg[l].astype(BF16), ffn1_wu=ffn1_wu[l].astype(BF16), ffn1_wd=ffn1_wd[l].astype(BF16),
            mix_norm=mix_norm[l].reshape(1, d).astype(F32),
            w_in=w_in[l].astype(BF16),
            gain=_gain_row(na_q_norm[l], na_k_norm[l], ga_q_norm[l], ga_k_norm[l], sw_q_norm[l], sw_k_norm[l]),
            na_bias=_na_bias_table(na_rel_bias[l]),
            sw_bias=sw_bias,
            sw_sink=_sw_sink_rows(sw_sink[l], sw_group),
            w_gate=w_gate[l].astype(BF16), w_branch=w_branch[l].astype(BF16), w_out=w_out[l].astype(BF16),
            ffn2_norm=ffn2_norm[l].reshape(1, d).astype(F32),
            ffn2_wg=ffn2_wg[l].astype(BF16), ffn2_wu=ffn2_wu[l].astype(BF16), ffn2_wd=ffn2_wd[l].astype(BF16),
        ))
    outs = []
    for x3 in (x_prompt, x_sample):
        outs.append(_trunk(x3, layers, _rope_tables(x3.shape[1])))
    return tuple(outs)
```

```python
import functools
import math

import numpy as np
import jax
import jax.numpy as jnp
from jax import lax
from jax.experimental import pallas as pl
from jax.experimental.pallas import tpu as pltpu

F32 = jnp.float32
BF16 = jnp.bfloat16

EPS = 1e-6
HEAD_DIM = 128
GRID_W = 64
NA_HEADS = 8
NA_ROWS = 8
NA_COLS = 16
GA_HEADS = 8
GA_KV_HEADS = 2
SW_HEADS = 8
SW_KV_HEADS = 2
WINDOW = 128
Q_BLOCK = 128
ROPE_THETA = 10000.0
MASKED = -1e30
LOG2E = math.log2(math.e)

NA_Q0, NA_K0, NA_V0 = 0, 8, 16
GA_Q0, GA_K0, GA_V0 = 24, 32, 34
SW_Q0, SW_K0, SW_V0 = 36, 44, 46
N_IN_HEADS = 48
D_IN = N_IN_HEADS * HEAD_DIM

QKV_SEGMENTS = (
    (NA_Q0, 8, True, False), (NA_K0, 8, True, False), (NA_V0, 8, False, False),
    (GA_Q0, 8, True, True), (GA_K0, 2, True, True), (GA_V0, 2, False, False),
    (SW_Q0, 8, True, False), (SW_K0, 2, True, False), (SW_V0, 2, False, False),
)

NA_QROWS = 8
NA_KROWS = 16
NA_TQ = NA_QROWS * GRID_W
NA_TK = NA_KROWS * GRID_W

VMEM_LIMIT_BYTES = 56 * 1024 * 1024


def _params(sem):
    return pltpu.CompilerParams(dimension_semantics=sem, vmem_limit_bytes=VMEM_LIMIT_BYTES)


def _rms(xf, g):
    ms = jnp.mean(xf * xf, axis=-1, keepdims=True)
    return xf * lax.rsqrt(ms + EPS) * g


def _dot(a, b):
    return jnp.dot(a, b, preferred_element_type=F32)


def _dot_nt(a, b):
    return lax.dot_general(a, b, (((1,), (1,)), ((), ())), preferred_element_type=F32)


def _pick_tile(n, want):
    t = min(n, want)
    while n % t:
        t //= 2
    return t


def _ffn_kernel(x_ref, g_ref, wg_ref, wu_ref, wd_ref, o_ref, h_ref, acc_ref):
    j = pl.program_id(1)

    @pl.when(j == 0)
    def _():
        h_ref[...] = _rms(x_ref[...], g_ref[...]).astype(BF16)
        acc_ref[...] = jnp.zeros(acc_ref.shape, F32)

    h = h_ref[...]
    a = _dot(h, wg_ref[...])
    u = _dot(h, wu_ref[...])
    act = (a * jax.nn.sigmoid(a) * u).astype(BF16)
    acc_ref[...] += _dot(act, wd_ref[...])

    @pl.when(j == pl.num_programs(1) - 1)
    def _():
        o_ref[...] = x_ref[...] + 0.5 * acc_ref[...]


def _ffn(x, g, wg, wu, wd):
    t, d = x.shape
    f = wg.shape[1]
    tm = _pick_tile(t, 512)
    tf = _pick_tile(f, 512)
    return pl.pallas_call(
        _ffn_kernel,
        out_shape=jax.ShapeDtypeStruct((t, d), F32),
        grid=(t // tm, f // tf),
        in_specs=[
            pl.BlockSpec((tm, d), lambda i, j: (i, 0)),
            pl.BlockSpec((1, d), lambda i, j: (0, 0)),
            pl.BlockSpec((d, tf), lambda i, j: (0, j)),
            pl.BlockSpec((d, tf), lambda i, j: (0, j)),
            pl.BlockSpec((tf, d), lambda i, j: (j, 0)),
        ],
        out_specs=pl.BlockSpec((tm, d), lambda i, j: (i, 0)),
        scratch_shapes=[pltpu.VMEM((tm, d), BF16), pltpu.VMEM((tm, d), F32)],
        compiler_params=_params(("parallel", "arbitrary")),
        name="ffn",
    )(x, g, wg, wu, wd)


def _swap_rotary_halves(y):
    lane = lax.broadcasted_iota(jnp.int32, y.shape, 1)
    first = (lane & 32) == 0
    return jnp.where(first, pltpu.roll(y, 96, 1), pltpu.roll(y, 32, 1))


def _qkv_kernel(x_ref, g_ref, w_ref, gain_ref, cos_ref, sin_ref, o_ref):
    h = _rms(x_ref[...], g_ref[...]).astype(BF16)
    for h0, nh, norm, rope in QKV_SEGMENTS:
        c0 = h0 * HEAD_DIM
        y = _dot(h, w_ref[:, c0:c0 + nh * HEAD_DIM])
        for hh in range(nh):
            lo = hh * HEAD_DIM
            yh = y[:, lo:lo + HEAD_DIM]
            if norm:
                yh = _rms(yh, gain_ref[:, c0 + lo:c0 + lo + HEAD_DIM])
            if rope:
                yh = yh * cos_ref[...] + _swap_rotary_halves(yh) * sin_ref[...]
            o_ref[h0 + hh] = yh.astype(BF16)


def _qkv(x, g, w_in, gain, cos, sin, seq):
    t, d = x.shape
    tm = _pick_tile(seq, 256)
    nseq = seq // tm
    return pl.pallas_call(
        _qkv_kernel,
        out_shape=jax.ShapeDtypeStruct((N_IN_HEADS, t, HEAD_DIM), BF16),
        grid=(t // tm,),
        in_specs=[
            pl.BlockSpec((tm, d), lambda i: (i, 0)),
            pl.BlockSpec((1, d), lambda i: (0, 0)),
            pl.BlockSpec((d, D_IN), lambda i: (0, 0), pipeline_mode=pl.Buffered(1)),
            pl.BlockSpec((1, D_IN), lambda i: (0, 0)),
            pl.BlockSpec((tm, HEAD_DIM), lambda i: (i % nseq, 0)),
            pl.BlockSpec((tm, HEAD_DIM), lambda i: (i % nseq, 0)),
        ],
        out_specs=pl.BlockSpec((N_IN_HEADS, tm, HEAD_DIM), lambda i: (0, i, 0)),
        compiler_params=_params(("parallel",)),
        name="qkv",
    )(x, g, w_in, gain, cos, sin)


def _edge_variant(i, nblk):
    return jnp.where(i == 0, 0, jnp.where(i == nblk - 1, 2, 1))


def _two_stage_loop(nblk, logits, finish, sa_ref, sb_ref):
    logits(0, sa_ref)

    def body(j, carry):
        i = 2 * j
        logits(i + 1, sb_ref)
        finish(i, sa_ref)
        logits(i + 2, sa_ref)
        finish(i + 1, sb_ref)
        return carry

    lax.fori_loop(0, nblk // 2 - 1, body, 0)
    logits(nblk - 1, sb_ref)
    finish(nblk - 2, sa_ref)
    finish(nblk - 1, sb_ref)


def _na_kernel(q_ref, k_ref, v_ref, bias_ref, o_ref, v1_ref, sa_ref, sb_ref, *, rows):
    nblk = q_ref.shape[0] // NA_TQ
    v1_ref[:, :HEAD_DIM] = v_ref[...]
    v1_ref[:, HEAD_DIM:] = jnp.ones(v_ref.shape, BF16)

    def window(i):
        start_row = jnp.clip(i * NA_QROWS - NA_ROWS // 2, 0, rows - NA_KROWS)
        return pl.multiple_of(start_row * GRID_W, GRID_W)

    def logits(i, dst_ref):
        q = q_ref[pl.ds(pl.multiple_of(i * NA_TQ, NA_TQ), NA_TQ), :]
        dst_ref[...] = _dot_nt(q, k_ref[pl.ds(window(i), NA_TK), :])

    def finish(i, src_ref):
        s = src_ref[...] + bias_ref[_edge_variant(i, nblk)]
        m = jnp.max(s, axis=-1, keepdims=True)
        p = jnp.exp2(s - m)
        pv = _dot(p.astype(BF16), v1_ref[pl.ds(window(i), NA_TK), :])
        o = pv[:, :HEAD_DIM] / pv[:, HEAD_DIM:]
        o_ref[pl.ds(pl.multiple_of(i * NA_TQ, NA_TQ), NA_TQ), :] = o.astype(BF16)

    _two_stage_loop(nblk, logits, finish, sa_ref, sb_ref)


def _na(qkv, bias, batch, seq):
    rows = seq // GRID_W
    assert seq % (2 * NA_TQ) == 0 and rows >= NA_KROWS
    return pl.pallas_call(
        functools.partial(_na_kernel, rows=rows),
        out_shape=jax.ShapeDtypeStruct((NA_HEADS, batch, seq, HEAD_DIM), BF16),
        grid=(NA_HEADS, batch),
        in_specs=[
            pl.BlockSpec((None, None, seq, HEAD_DIM), lambda h, b: (NA_Q0 + h, b, 0, 0)),
            pl.BlockSpec((None, None, seq, HEAD_DIM), lambda h, b: (NA_K0 + h, b, 0, 0)),
            pl.BlockSpec((None, None, seq, HEAD_DIM), lambda h, b: (NA_V0 + h, b, 0, 0)),
            pl.BlockSpec((None, 3, NA_TQ, NA_TK), lambda h, b: (h, 0, 0, 0)),
        ],
        out_specs=pl.BlockSpec((None, None, seq, HEAD_DIM), lambda h, b: (h, b, 0, 0)),
        scratch_shapes=[
            pltpu.VMEM((seq, 2 * HEAD_DIM), BF16),
            pltpu.VMEM((NA_TQ, NA_TK), F32),
            pltpu.VMEM((NA_TQ, NA_TK), F32),
        ],
        compiler_params=_params(("parallel", "parallel")),
        name="na",
    )(qkv, qkv, qkv, bias)


def _na_bias_table(rel_bias):
    rel = rel_bias.astype(F32) * LOG2E
    rel_pad = jnp.pad(rel, ((0, 0), (0, 0), (GRID_W, GRID_W)))
    off = GRID_W + NA_COLS - 1
    t1 = jnp.stack([rel_pad[:, :, off - c:off - c + GRID_W] for c in range(GRID_W)], axis=1)
    cq = np.arange(GRID_W)[:, None]
    kc = np.arange(GRID_W)[None, :]
    cs = np.clip(cq - NA_COLS // 2, 0, GRID_W - NA_COLS)
    col_ok = (kc >= cs) & (kc < cs + NA_COLS)
    t1 = jnp.where(col_ok[None, :, None, :], t1, MASKED)

    rows = 3 * NA_QROWS
    blocks = []
    for v in range(3):
        start_row = int(np.clip(v * NA_QROWS - NA_ROWS // 2, 0, rows - NA_KROWS))
        for qr in range(NA_QROWS):
            r = v * NA_QROWS + qr
            rs = int(np.clip(r - NA_ROWS // 2, 0, rows - NA_ROWS))
            lo = rs - start_row
            dr0 = rs - r + NA_ROWS - 1
            blk = t1[:, :, dr0:dr0 + NA_ROWS, :]
            blocks.append(jnp.pad(blk, ((0, 0), (0, 0), (lo, NA_KROWS - NA_ROWS - lo), (0, 0)),
                                  constant_values=MASKED))
    table = jnp.stack(blocks, axis=1)
    return table.reshape(NA_HEADS, 3, NA_TQ, NA_TK)


def _ga_kernel(q_ref, k_ref, v_ref, o_ref, v1_ref, m_ref, acc_ref, sa_ref, sb_ref, *, tq, tk, group):
    @pl.when(pl.program_id(2) == 0)
    def _():
        v1_ref[:, :HEAD_DIM] = v_ref[...]
        v1_ref[:, HEAD_DIM:] = jnp.ones(v_ref.shape, BF16)

    m_ref[...] = jnp.full(m_ref.shape, MASKED, F32)
    acc_ref[...] = jnp.zeros(acc_ref.shape, F32)
    lane_tiles = tk // HEAD_DIM
    n_chunks = k_ref.shape[0] // tk

    def logits(c, dst_ref):
        start = pl.multiple_of(c * tk, tk)
        q = q_ref[...].reshape(group * tq, HEAD_DIM)
        dst_ref[...] = _dot_nt(q, k_ref[pl.ds(start, tk), :])

    def accumulate(c, src_ref):
        start = pl.multiple_of(c * tk, tk)
        s = src_ref[...]
        m_prev = m_ref[...]
        m_new = jnp.maximum(m_prev, jnp.max(s, axis=-1, keepdims=True))
        alpha = jnp.exp2(m_prev - m_new)
        p = jnp.exp2(s - jnp.concatenate([m_new] * lane_tiles, axis=1))
        pv = _dot(p.astype(BF16), v1_ref[pl.ds(start, tk), :])
        acc_ref[...] = jnp.concatenate([alpha, alpha], axis=1) * acc_ref[...] + pv
        m_ref[...] = m_new

    _two_stage_loop(n_chunks, logits, accumulate, sa_ref, sb_ref)
    o = acc_ref[:, :HEAD_DIM] / acc_ref[:, HEAD_DIM:]
    o_ref[...] = o.reshape(group, tq, HEAD_DIM).astype(BF16)


def _ga(qkv, batch, seq):
    group = GA_HEADS // GA_KV_HEADS
    tq = _pick_tile(seq, 256)
    tk = _pick_tile(seq, 512)
    assert (seq // tk) % 2 == 0
    return pl.pallas_call(
        functools.partial(_ga_kernel, tq=tq, tk=tk, group=group),
        out_shape=jax.ShapeDtypeStruct((GA_HEADS, batch, seq, HEAD_DIM), BF16),
        grid=(batch, GA_KV_HEADS, seq // tq),
        in_specs=[
            pl.BlockSpec((group, None, tq, HEAD_DIM), lambda b, kv, i: (GA_Q0 // group + kv, b, i, 0)),
            pl.BlockSpec((None, None, seq, HEAD_DIM), lambda b, kv, i: (GA_K0 + kv, b, 0, 0)),
            pl.BlockSpec((None, None, seq, HEAD_DIM), lambda b, kv, i: (GA_V0 + kv, b, 0, 0)),
        ],
        out_specs=pl.BlockSpec((group, None, tq, HEAD_DIM), lambda b, kv, i: (kv, b, i, 0)),
        scratch_shapes=[
            pltpu.VMEM((seq, 2 * HEAD_DIM), BF16),
            pltpu.VMEM((group * tq, HEAD_DIM), F32),
            pltpu.VMEM((group * tq, 2 * HEAD_DIM), F32),
            pltpu.VMEM((group * tq, tk), F32),
            pltpu.VMEM((group * tq, tk), F32),
        ],
        compiler_params=_params(("parallel", "parallel", "arbitrary")),
        name="ga",
    )(qkv, qkv, qkv)


def _sw_kernel(q_ref, k_ref, v_ref, bias_ref, sink_ref, o_ref, v1_ref, sa_ref, sb_ref, *, seq, group):
    span = Q_BLOCK + 2 * WINDOW
    nblk_seq = seq // Q_BLOCK
    nblk = q_ref.shape[1] // Q_BLOCK
    first = pl.program_id(2) * nblk

    @pl.when(pl.program_id(2) == 0)
    def _():
        v1_ref[:, :HEAD_DIM] = v_ref[...]
        v1_ref[:, HEAD_DIM:] = jnp.ones(v_ref.shape, BF16)

    def window(n):
        return pl.multiple_of(jnp.clip((first + n) * Q_BLOCK - WINDOW, 0, seq - span), Q_BLOCK)

    def rows(n):
        return pl.ds(pl.multiple_of(n * Q_BLOCK, Q_BLOCK), Q_BLOCK)

    def logits(n, dst_ref):
        q = q_ref[:, rows(n), :].reshape(group * Q_BLOCK, HEAD_DIM)
        dst_ref[...] = _dot_nt(q, k_ref[pl.ds(window(n), span), :])

    def finish(n, src_ref):
        s = src_ref[...] + bias_ref[_edge_variant(first + n, nblk_seq)]
        sink = sink_ref[...]
        m = jnp.maximum(jnp.max(s, axis=-1, keepdims=True), sink)
        p = jnp.exp2(s - jnp.concatenate([m] * (span // HEAD_DIM), axis=1))
        pv = _dot(p.astype(BF16), v1_ref[pl.ds(window(n), span), :])
        o = pv[:, :HEAD_DIM] / (pv[:, HEAD_DIM:] + jnp.exp2(sink - m))
        o_ref[:, rows(n), :] = o.reshape(group, Q_BLOCK, HEAD_DIM).astype(BF16)

    _two_stage_loop(nblk, logits, finish, sa_ref, sb_ref)


def _sw_bias_table():
    group = SW_HEADS // SW_KV_HEADS
    span = Q_BLOCK + 2 * WINDOW
    slopes = 2.0 ** (-8.0 * np.arange(1, SW_HEADS + 1, dtype=np.float64) / SW_HEADS)
    qi = np.arange(Q_BLOCK)[:, None]
    kj = np.arange(span)[None, :]
    table = np.empty((SW_KV_HEADS, 3, group, Q_BLOCK, span), np.float32)
    for v in range(3):
        dist = np.abs(qi - (kj - v * WINDOW))
        for h in range(SW_HEADS):
            table[h // group, v, h % group] = np.where(dist <= WINDOW, -LOG2E * slopes[h] * dist, MASKED)
    return jnp.asarray(table.reshape(SW_KV_HEADS, 3, group * Q_BLOCK, span))


def _sw(qkv, bias, sink, batch, seq):
    group = SW_HEADS // SW_KV_HEADS
    span = Q_BLOCK + 2 * WINDOW
    tq = _pick_tile(seq, 16 * Q_BLOCK)
    assert tq % (2 * Q_BLOCK) == 0 and seq >= span
    rows = group * Q_BLOCK
    return pl.pallas_call(
        functools.partial(_sw_kernel, seq=seq, group=group),
        out_shape=jax.ShapeDtypeStruct((SW_HEADS, batch, seq, HEAD_DIM), BF16),
        grid=(batch, SW_KV_HEADS, seq // tq),
        in_specs=[
            pl.BlockSpec((group, None, tq, HEAD_DIM), lambda b, kv, i: (SW_Q0 // group + kv, b, i, 0)),
            pl.BlockSpec((None, None, seq, HEAD_DIM), lambda b, kv, i: (SW_K0 + kv, b, 0, 0)),
            pl.BlockSpec((None, None, seq, HEAD_DIM), lambda b, kv, i: (SW_V0 + kv, b, 0, 0)),
            pl.BlockSpec((None, 3, rows, span), lambda b, kv, i: (kv, 0, 0, 0)),
            pl.BlockSpec((None, rows, HEAD_DIM), lambda b, kv, i: (kv, 0, 0)),
        ],
        out_specs=pl.BlockSpec((group, None, tq, HEAD_DIM), lambda b, kv, i: (kv, b, i, 0)),
        scratch_shapes=[
            pltpu.VMEM((seq, 2 * HEAD_DIM), BF16),
            pltpu.VMEM((rows, span), F32),
            pltpu.VMEM((rows, span), F32),
        ],
        compiler_params=_params(("parallel", "parallel", "arbitrary")),
        name="sw",
    )(qkv, qkv, qkv, bias, sink)


def _merge_kernel(x_ref, g_ref, oa_ref, ob_ref, oc_ref, wg_ref, wb_ref, wo_ref, o_ref, h_ref, merged_ref):
    j = pl.program_id(1)

    @pl.when(j == 0)
    def _():
        h_ref[...] = _rms(x_ref[...], g_ref[...]).astype(BF16)

    h = h_ref[...]
    merged = None
    for n, br_ref in enumerate((oa_ref, ob_ref, oc_ref)):
        branch = jnp.concatenate([br_ref[hd] for hd in range(br_ref.shape[0])], axis=1)
        term = jax.nn.sigmoid(_dot(h, wg_ref[n])) * _dot(branch, wb_ref[n])
        merged = term if merged is None else merged + term
    merged_ref[j] = merged.astype(BF16)

    @pl.when(j == pl.num_programs(1) - 1)
    def _():
        merged_all = jnp.concatenate([merged_ref[c] for c in range(merged_ref.shape[0])], axis=1)
        o_ref[...] = x_ref[...] + _dot(merged_all, wo_ref[...])


def _merge(x, g, oa, ob, oc, w_gate, w_branch, w_out):
    t, d = x.shape
    nh = oa.shape[0]
    bw = nh * HEAD_DIM
    tm = _pick_tile(t, 512)
    tn = _pick_tile(d, 256)
    return pl.pallas_call(
        _merge_kernel,
        out_shape=jax.ShapeDtypeStruct((t, d), F32),
        grid=(t // tm, d // tn),
        in_specs=[
            pl.BlockSpec((tm, d), lambda i, j: (i, 0)),
            pl.BlockSpec((1, d), lambda i, j: (0, 0)),
            pl.BlockSpec((nh, tm, HEAD_DIM), lambda i, j: (0, i, 0)),
            pl.BlockSpec((nh, tm, HEAD_DIM), lambda i, j: (0, i, 0)),
            pl.BlockSpec((nh, tm, HEAD_DIM), lambda i, j: (0, i, 0)),
            pl.BlockSpec((3, d, tn), lambda i, j: (0, 0, j)),
            pl.BlockSpec((3, bw, tn), lambda i, j: (0, 0, j)),
            pl.BlockSpec((d, d), lambda i, j: (0, 0), pipeline_mode=pl.Buffered(1)),
        ],
        out_specs=pl.BlockSpec((tm, d), lambda i, j: (i, 0)),
        scratch_shapes=[pltpu.VMEM((tm, d), BF16), pltpu.VMEM((d // tn, tm, tn), BF16)],
        compiler_params=_params(("parallel", "arbitrary")),
        name="merge",
    )(x, g, oa, ob, oc, w_gate, w_branch, w_out)


def _rope_tables(seq):
    quarter = HEAD_DIM // 4
    freqs = ROPE_THETA ** (-jnp.arange(quarter, dtype=F32) / quarter)
    t = jnp.arange(seq)
    row = (t // GRID_W).astype(F32)
    col = (t % GRID_W).astype(F32)
    ang_r = row[:, None] * freqs[None, :]
    ang_c = col[:, None] * freqs[None, :]
    cos = jnp.concatenate([jnp.cos(ang_r)] * 2 + [jnp.cos(ang_c)] * 2, axis=-1)
    sin = jnp.concatenate([-jnp.sin(ang_r), jnp.sin(ang_r), -jnp.sin(ang_c), jnp.sin(ang_c)], axis=-1)
    return cos, sin


def _gain_row(na_q, na_k, ga_q, ga_k, sw_q, sw_k):
    scale = HEAD_DIM ** -0.5 * LOG2E
    one = jnp.ones((HEAD_DIM,), F32)
    parts = ([na_q * scale] * 8 + [na_k] * 8 + [one] * 8 + [ga_q * scale] * 8 + [ga_k] * 2 + [one] * 2
             + [sw_q * scale] * 8 + [sw_k] * 2 + [one] * 2)
    return jnp.concatenate([p.astype(F32) for p in parts]).reshape(1, D_IN)


def _sw_sink_rows(sink, group):
    per_head = (sink.astype(F32) * LOG2E).reshape(SW_KV_HEADS, group, 1, 1)
    return jnp.broadcast_to(per_head, (SW_KV_HEADS, group, Q_BLOCK, HEAD_DIM)).reshape(
        SW_KV_HEADS, group * Q_BLOCK, HEAD_DIM)


def _trunk(x3, layers, rope):
    batch, seq, d = x3.shape
    x = x3.reshape(batch * seq, d)
    cos, sin = rope
    for p in layers:
        x = _ffn(x, p["ffn1_norm"], p["ffn1_wg"], p["ffn1_wu"], p["ffn1_wd"])
        qkv = _qkv(x, p["mix_norm"], p["w_in"], p["gain"], cos, sin, seq)
        qkv = qkv.reshape(N_IN_HEADS, batch, seq, HEAD_DIM)
        oa = _na(qkv, p["na_bias"], batch, seq).reshape(NA_HEADS, batch * seq, HEAD_DIM)
        ob = _ga(qkv, batch, seq).reshape(GA_HEADS, batch * seq, HEAD_DIM)
        oc = _sw(qkv, p["sw_bias"], p["sw_sink"], batch, seq).reshape(SW_HEADS, batch * seq, HEAD_DIM)
        x = _merge(x, p["mix_norm"], oa, ob, oc, p["w_gate"], p["w_branch"], p["w_out"])
        x = _ffn(x, p["ffn2_norm"], p["ffn2_wg"], p["ffn2_wu"], p["ffn2_wd"])
    return x.reshape(batch, seq, d)


def kernel(x_prompt, x_sample, ffn1_norm, ffn1_wg, ffn1_wu, ffn1_wd, mix_norm, w_in, w_gate, na_q_norm, na_k_norm, na_rel_bias, ga_q_norm, ga_k_norm, sw_q_norm, sw_k_norm, sw_sink, w_branch, w_out, ffn2_norm, ffn2_wg, ffn2_wu, ffn2_wd):
    depth = w_in.shape[0]
    d = x_prompt.shape[-1]
    sw_group = SW_HEADS // SW_KV_HEADS
    sw_bias = _sw_bias_table()
    layers = []
    for l in range(depth):
        layers.append(dict(
            ffn1_norm=ffn1_norm[l].reshape(1, d).astype(F32),
            ffn1_wg=ffn1_wg[l].astype(BF16), ffn1_wu=ffn1_wu[l].astype(BF16), ffn1_wd=ffn1_wd[l].astype(BF16),
            mix_norm=mix_norm[l].reshape(1, d).astype(F32),
            w_in=w_in[l].astype(BF16),
            gain=_gain_row(na_q_norm[l], na_k_norm[l], ga_q_norm[l], ga_k_norm[l], sw_q_norm[l], sw_k_norm[l]),
            na_bias=_na_bias_table(na_rel_bias[l]),
            sw_bias=sw_bias,
            sw_sink=_sw_sink_rows(sw_sink[l], sw_group),
            w_gate=w_gate[l].astype(BF16), w_branch=w_branch[l].astype(BF16), w_out=w_out[l].astype(BF16),
            ffn2_norm=ffn2_norm[l].reshape(1, d).astype(F32),
            ffn2_wg=ffn2_wg[l].astype(BF16), ffn2_wu=ffn2_wu[l].astype(BF16), ffn2_wd=ffn2_wd[l].astype(BF16),
        ))
    outs = []
    for x3 in (x_prompt, x_sample):
        outs.append(_trunk(x3, layers, _rope_tables(x3.shape[1])))
    return tuple(outs)
```

```python
import functools
import math

import numpy as np
import jax
import jax.numpy as jnp
from jax import lax
from jax.experimental import pallas as pl
from jax.experimental.pallas import tpu as pltpu

F32 = jnp.float32
BF16 = jnp.bfloat16

EPS = 1e-6
HEAD_DIM = 128
GRID_W = 64
NA_HEADS = 8
NA_ROWS = 8
NA_COLS = 16
GA_HEADS = 8
GA_KV_HEADS = 2
SW_HEADS = 8
SW_KV_HEADS = 2
WINDOW = 128
Q_BLOCK = 128
ROPE_THETA = 10000.0
MASKED = -1e30
LOG2E = math.log2(math.e)
MAX_EXP2_RANGE = 100.0

N_IN_HEADS = 48
D_IN = N_IN_HEADS * HEAD_DIM

NA_Q0, GA_Q0, SW_Q0, NA_V0, GA_V0, SW_V0 = 0, 8, 16, 24, 32, 34
N_QV = 36
NA_K0, GA_K0, SW_K0 = 0, 8, 10
N_KT = 12

QKV_SEGMENTS = (
    (0, 8, True, False, False, NA_Q0), (8, 8, True, False, True, NA_K0), (16, 8, False, False, False, NA_V0),
    (24, 8, True, True, False, GA_Q0), (32, 2, True, True, True, GA_K0), (34, 2, False, False, False, GA_V0),
    (36, 8, True, False, False, SW_Q0), (44, 2, True, False, True, SW_K0), (46, 2, False, False, False, SW_V0),
)

NA_QROWS = 8
NA_KROWS = 16
NA_TQ = NA_QROWS * GRID_W
NA_TK = NA_KROWS * GRID_W

VMEM_LIMIT_BYTES = 56 * 1024 * 1024


def _params(sem):
    return pltpu.CompilerParams(dimension_semantics=sem, vmem_limit_bytes=VMEM_LIMIT_BYTES)


def _rms(xf, g):
    ms = jnp.mean(xf * xf, axis=-1, keepdims=True)
    return xf * lax.rsqrt(ms + EPS) * g


def _dot(a, b):
    return jnp.dot(a, b, preferred_element_type=F32)


def _pick_tile(n, want):
    t = min(n, want)
    while n % t:
        t //= 2
    return t


def _ffn_kernel(x_ref, g_ref, wg_ref, wu_ref, wd_ref, o_ref, h_ref, acc_ref):
    j = pl.program_id(1)

    @pl.when(j == 0)
    def _():
        h_ref[...] = _rms(x_ref[...], g_ref[...]).astype(BF16)
        acc_ref[...] = jnp.zeros(acc_ref.shape, F32)

    h = h_ref[...]
    a = _dot(h, wg_ref[...])
    u = _dot(h, wu_ref[...])
    act = (a * jax.nn.sigmoid(a) * u).astype(BF16)
    acc_ref[...] += _dot(act, wd_ref[...])

    @pl.when(j == pl.num_programs(1) - 1)
    def _():
        o_ref[...] = x_ref[...] + 0.5 * acc_ref[...]


def _ffn(x, g, wg, wu, wd):
    t, d = x.shape
    f = wg.shape[1]
    tm = _pick_tile(t, 512)
    tf = _pick_tile(f, 512)
    return pl.pallas_call(
        _ffn_kernel,
        out_shape=jax.ShapeDtypeStruct((t, d), F32),
        grid=(t // tm, f // tf),
        in_specs=[
            pl.BlockSpec((tm, d), lambda i, j: (i, 0)),
            pl.BlockSpec((1, d), lambda i, j: (0, 0)),
            pl.BlockSpec((d, tf), lambda i, j: (0, j)),
            pl.BlockSpec((d, tf), lambda i, j: (0, j)),
            pl.BlockSpec((tf, d), lambda i, j: (j, 0)),
        ],
        out_specs=pl.BlockSpec((tm, d), lambda i, j: (i, 0)),
        scratch_shapes=[pltpu.VMEM((tm, d), BF16), pltpu.VMEM((tm, d), F32)],
        compiler_params=_params(("parallel", "arbitrary")),
        name="ffn",
    )(x, g, wg, wu, wd)


def _swap_rotary_halves(y):
    lane = lax.broadcasted_iota(jnp.int32, y.shape, 1)
    first = (lane & 32) == 0
    return jnp.where(first, pltpu.roll(y, 96, 1), pltpu.roll(y, 32, 1))


def _qkv_kernel(x_ref, g_ref, w_ref, gain_ref, cos_ref, sin_ref, qv_ref, kt_ref):
    h = _rms(x_ref[...], g_ref[...]).astype(BF16)
    for h0, nh, norm, rope, is_key, slot0 in QKV_SEGMENTS:
        c0 = h0 * HEAD_DIM
        y = _dot(h, w_ref[:, c0:c0 + nh * HEAD_DIM])
        for hh in range(nh):
            lo = hh * HEAD_DIM
            yh = y[:, lo:lo + HEAD_DIM]
            if norm:
                yh = _rms(yh, gain_ref[:, c0 + lo:c0 + lo + HEAD_DIM])
            if rope:
                yh = yh * cos_ref[...] + _swap_rotary_halves(yh) * sin_ref[...]
            if is_key:
                kt_ref[slot0 + hh] = yh.T.astype(BF16)
            else:
                qv_ref[slot0 + hh] = yh.astype(BF16)


def _qkv(x, g, w_in, gain, cos, sin, seq):
    t, d = x.shape
    tm = _pick_tile(seq, 256)
    nseq = seq // tm
    return pl.pallas_call(
        _qkv_kernel,
        out_shape=(jax.ShapeDtypeStruct((N_QV, t, HEAD_DIM), BF16),
                   jax.ShapeDtypeStruct((N_KT, HEAD_DIM, t), BF16)),
        grid=(t // tm,),
        in_specs=[
            pl.BlockSpec((tm, d), lambda i: (i, 0)),
            pl.BlockSpec((1, d), lambda i: (0, 0)),
            pl.BlockSpec((d, D_IN), lambda i: (0, 0), pipeline_mode=pl.Buffered(1)),
            pl.BlockSpec((1, D_IN), lambda i: (0, 0)),
            pl.BlockSpec((tm, HEAD_DIM), lambda i: (i % nseq, 0)),
            pl.BlockSpec((tm, HEAD_DIM), lambda i: (i % nseq, 0)),
        ],
        out_specs=(pl.BlockSpec((N_QV, tm, HEAD_DIM), lambda i: (0, i, 0)),
                   pl.BlockSpec((N_KT, HEAD_DIM, tm), lambda i: (0, 0, i))),
        compiler_params=_params(("parallel",)),
        name="qkv",
    )(x, g, w_in, gain, cos, sin)


def _edge_variant(i, nblk):
    return jnp.where(i == 0, 0, jnp.where(i == nblk - 1, 2, 1))


def _two_stage_loop(nblk, logits, finish, sa_ref, sb_ref):
    logits(0, sa_ref)

    def body(j, carry):
        i = 2 * j
        logits(i + 1, sb_ref)
        finish(i, sa_ref)
        logits(i + 2, sa_ref)
        finish(i + 1, sb_ref)
        return carry

    lax.fori_loop(0, nblk // 2 - 1, body, 0)
    logits(nblk - 1, sb_ref)
    finish(nblk - 2, sa_ref)
    finish(nblk - 1, sb_ref)


def _na_kernel(q_ref, kt_ref, v_ref, bias_ref, o_ref, v1_ref, sa_ref, sb_ref, *, rows):
    nblk = q_ref.shape[0] // NA_TQ
    v1_ref[:, :HEAD_DIM] = v_ref[...]
    v1_ref[:, HEAD_DIM:] = jnp.ones(v_ref.shape, BF16)

    def window(i):
        start_row = jnp.clip(i * NA_QROWS - NA_ROWS // 2, 0, rows - NA_KROWS)
        return pl.multiple_of(start_row * GRID_W, (NA_ROWS // 2) * GRID_W)

    def logits(i, dst_ref):
        q = q_ref[pl.ds(pl.multiple_of(i * NA_TQ, NA_TQ), NA_TQ), :]
        dst_ref[...] = _dot(q, kt_ref[:, pl.ds(window(i), NA_TK)])

    def finish(i, src_ref):
        s = src_ref[...] + bias_ref[_edge_variant(i, nblk)]
        m = jnp.max(s, axis=-1, keepdims=True)
        p = jnp.exp2(s - m).astype(BF16)
        pv = _dot(p, v1_ref[pl.ds(window(i), NA_TK), :])
        o = pv[:, :HEAD_DIM] / pv[:, HEAD_DIM:]
        o_ref[pl.ds(pl.multiple_of(i * NA_TQ, NA_TQ), NA_TQ), :] = o.astype(BF16)

    _two_stage_loop(nblk, logits, finish, sa_ref, sb_ref)


def _na(qv, kt, bias, batch, seq):
    rows = seq // GRID_W
    assert seq % (2 * NA_TQ) == 0 and rows >= NA_KROWS
    return pl.pallas_call(
        functools.partial(_na_kernel, rows=rows),
        out_shape=jax.ShapeDtypeStruct((NA_HEADS, batch, seq, HEAD_DIM), BF16),
        grid=(NA_HEADS, batch),
        in_specs=[
            pl.BlockSpec((None, None, seq, HEAD_DIM), lambda h, b: (NA_Q0 + h, b, 0, 0)),
            pl.BlockSpec((None, HEAD_DIM, seq), lambda h, b: (NA_K0 + h, 0, b)),
            pl.BlockSpec((None, None, seq, HEAD_DIM), lambda h, b: (NA_V0 + h, b, 0, 0)),
            pl.BlockSpec((None, 3, NA_TQ, NA_TK), lambda h, b: (h, 0, 0, 0)),
        ],
        out_specs=pl.BlockSpec((None, None, seq, HEAD_DIM), lambda h, b: (h, b, 0, 0)),
        scratch_shapes=[
            pltpu.VMEM((seq, 2 * HEAD_DIM), BF16),
            pltpu.VMEM((NA_TQ, NA_TK), F32),
            pltpu.VMEM((NA_TQ, NA_TK), F32),
        ],
        compiler_params=_params(("parallel", "parallel")),
        name="na",
    )(qv, kt, qv, bias)


def _na_bias_table(rel_bias):
    rel = rel_bias.astype(F32) * LOG2E
    rel_pad = jnp.pad(rel, ((0, 0), (0, 0), (GRID_W, GRID_W)))
    off = GRID_W + NA_COLS - 1
    t1 = jnp.stack([rel_pad[:, :, off - c:off - c + GRID_W] for c in range(GRID_W)], axis=1)
    cq = np.arange(GRID_W)[:, None]
    kc = np.arange(GRID_W)[None, :]
    cs = np.clip(cq - NA_COLS // 2, 0, GRID_W - NA_COLS)
    col_ok = (kc >= cs) & (kc < cs + NA_COLS)
    t1 = jnp.where(col_ok[None, :, None, :], t1, MASKED)

    rows = 3 * NA_QROWS
    blocks = []
    for v in range(3):
        start_row = int(np.clip(v * NA_QROWS - NA_ROWS // 2, 0, rows - NA_KROWS))
        for qr in range(NA_QROWS):
            r = v * NA_QROWS + qr
            rs = int(np.clip(r - NA_ROWS // 2, 0, rows - NA_ROWS))
            lo = rs - start_row
            dr0 = rs - r + NA_ROWS - 1
            blk = t1[:, :, dr0:dr0 + NA_ROWS, :]
            blocks.append(jnp.pad(blk, ((0, 0), (0, 0), (lo, NA_KROWS - NA_ROWS - lo), (0, 0)),
                                  constant_values=MASKED))
    table = jnp.stack(blocks, axis=1)
    return table.reshape(NA_HEADS, 3, NA_TQ, NA_TK)


def _ga_kernel(bound_ref, q_ref, kt_ref, v_ref, o_ref, v1_ref, m_ref, acc_ref, sa_ref, sb_ref,
               *, tq, tk, group, bounded):
    @pl.when(pl.program_id(2) == 0)
    def _():
        v1_ref[:, :HEAD_DIM] = v_ref[...]
        v1_ref[:, HEAD_DIM:] = jnp.ones(v_ref.shape, BF16)

    m_ref[...] = jnp.full(m_ref.shape, MASKED, F32)
    acc_ref[...] = jnp.zeros(acc_ref.shape, F32)
    lane_tiles = tk // HEAD_DIM
    n_chunks = kt_ref.shape[1] // tk

    def logits(c, dst_ref):
        start = pl.multiple_of(c * tk, tk)
        q = q_ref[...].reshape(group * tq, HEAD_DIM)
        s = _dot(q, kt_ref[:, pl.ds(start, tk)])
        if bounded:
            dst_ref[...] = jnp.exp2(s - bound_ref[0]).astype(BF16)
        else:
            dst_ref[...] = s

    def accumulate(c, src_ref):
        start = pl.multiple_of(c * tk, tk)
        if bounded:
            acc_ref[...] += _dot(src_ref[...], v1_ref[pl.ds(start, tk), :])
            return
        s = src_ref[...]
        m_prev = m_ref[...]
        m_new = jnp.maximum(m_prev, jnp.max(s, axis=-1, keepdims=True))
        alpha = jnp.exp2(m_prev - m_new)
        p = jnp.exp2(s - jnp.concatenate([m_new] * lane_tiles, axis=1))
        pv = _dot(p.astype(BF16), v1_ref[pl.ds(start, tk), :])
        acc_ref[...] = jnp.concatenate([alpha, alpha], axis=1) * acc_ref[...] + pv
        m_ref[...] = m_new

    _two_stage_loop(n_chunks, logits, accumulate, sa_ref, sb_ref)
    o = acc_ref[:, :HEAD_DIM] / acc_ref[:, HEAD_DIM:]
    o_ref[...] = o.reshape(group, tq, HEAD_DIM).astype(BF16)


def _ga(bound, qv, kt, batch, seq, bounded):
    group = GA_HEADS // GA_KV_HEADS
    tq = _pick_tile(seq, 256)
    tk = _pick_tile(seq, 512)
    assert (seq // tk) % 2 == 0
    score_dtype = BF16 if bounded else F32
    return pl.pallas_call(
        functools.partial(_ga_kernel, tq=tq, tk=tk, group=group, bounded=bounded),
        out_shape=jax.ShapeDtypeStruct((GA_HEADS, batch, seq, HEAD_DIM), BF16),
        grid=(batch, GA_KV_HEADS, seq // tq),
        in_specs=[
            pl.BlockSpec(memory_space=pltpu.SMEM),
            pl.BlockSpec((group, None, tq, HEAD_DIM), lambda b, kv, i: (GA_Q0 // group + kv, b, i, 0)),
            pl.BlockSpec((None, HEAD_DIM, seq), lambda b, kv, i: (GA_K0 + kv, 0, b)),
            pl.BlockSpec((None, None, seq, HEAD_DIM), lambda b, kv, i: (GA_V0 + kv, b, 0, 0)),
        ],
        out_specs=pl.BlockSpec((group, None, tq, HEAD_DIM), lambda b, kv, i: (kv, b, i, 0)),
        scratch_shapes=[
            pltpu.VMEM((seq, 2 * HEAD_DIM), BF16),
            pltpu.VMEM((group * tq, HEAD_DIM), F32),
            pltpu.VMEM((group * tq, 2 * HEAD_DIM), F32),
            pltpu.VMEM((group * tq, tk), score_dtype),
            pltpu.VMEM((group * tq, tk), score_dtype),
        ],
        compiler_params=_params(("parallel", "parallel", "arbitrary")),
        name="ga_bounded" if bounded else "ga",
    )(bound, qv, kt, qv)


def _sw_kernel(q_ref, kt_ref, v_ref, bias_ref, sink_ref, o_ref, v1_ref, sa_ref, sb_ref, *, seq, group):
    span = Q_BLOCK + 2 * WINDOW
    nblk_seq = seq // Q_BLOCK
    nblk = q_ref.shape[1] // Q_BLOCK
    first = pl.program_id(2) * nblk

    @pl.when(pl.program_id(2) == 0)
    def _():
        v1_ref[:, :HEAD_DIM] = v_ref[...]
        v1_ref[:, HEAD_DIM:] = jnp.ones(v_ref.shape, BF16)

    def window(n):
        return pl.multiple_of(jnp.clip((first + n) * Q_BLOCK - WINDOW, 0, seq - span), Q_BLOCK)

    def rows(n):
        return pl.ds(pl.multiple_of(n * Q_BLOCK, Q_BLOCK), Q_BLOCK)

    def logits(n, dst_ref):
        q = q_ref[:, rows(n), :].reshape(group * Q_BLOCK, HEAD_DIM)
        dst_ref[...] = _dot(q, kt_ref[:, pl.ds(window(n), span)])

    def finish(n, src_ref):
        sink = sink_ref[...]
        s = src_ref[...] + bias_ref[_edge_variant(first + n, nblk_seq)]
        m = jnp.maximum(jnp.max(s, axis=-1, keepdims=True), sink)
        p = jnp.exp2(s - jnp.concatenate([m] * (span // HEAD_DIM), axis=1)).astype(BF16)
        pv = _dot(p, v1_ref[pl.ds(window(n), span), :])
        o = pv[:, :HEAD_DIM] / (pv[:, HEAD_DIM:] + jnp.exp2(sink - m))
        o_ref[:, rows(n), :] = o.reshape(group, Q_BLOCK, HEAD_DIM).astype(BF16)

    _two_stage_loop(nblk, logits, finish, sa_ref, sb_ref)


def _sw_bias_table():
    group = SW_HEADS // SW_KV_HEADS
    span = Q_BLOCK + 2 * WINDOW
    slopes = 2.0 ** (-8.0 * np.arange(1, SW_HEADS + 1, dtype=np.float64) / SW_HEADS)
    qi = np.arange(Q_BLOCK)[:, None]
    kj = np.arange(span)[None, :]
    table = np.empty((SW_KV_HEADS, 3, group, Q_BLOCK, span), np.float32)
    for v in range(3):
        dist = np.abs(qi - (kj - v * WINDOW))
        for h in range(SW_HEADS):
            table[h // group, v, h % group] = np.where(dist <= WINDOW, -LOG2E * slopes[h] * dist, MASKED)
    return jnp.asarray(table.reshape(SW_KV_HEADS, 3, group * Q_BLOCK, span))


def _sw(qv, kt, bias, sink, batch, seq):
    group = SW_HEADS // SW_KV_HEADS
    span = Q_BLOCK + 2 * WINDOW
    tq = _pick_tile(seq, 16 * Q_BLOCK)
    assert tq % (2 * Q_BLOCK) == 0 and seq >= span
    rows = group * Q_BLOCK
    return pl.pallas_call(
        functools.partial(_sw_kernel, seq=seq, group=group),
        out_shape=jax.ShapeDtypeStruct((SW_HEADS, batch, seq, HEAD_DIM), BF16),
        grid=(batch, SW_KV_HEADS, seq // tq),
        in_specs=[
            pl.BlockSpec((group, None, tq, HEAD_DIM), lambda b, kv, i: (SW_Q0 // group + kv, b, i, 0)),
            pl.BlockSpec((None, HEAD_DIM, seq), lambda b, kv, i: (SW_K0 + kv, 0, b)),
            pl.BlockSpec((None, None, seq, HEAD_DIM), lambda b, kv, i: (SW_V0 + kv, b, 0, 0)),
            pl.BlockSpec((None, 3, rows, span), lambda b, kv, i: (kv, 0, 0, 0)),
            pl.BlockSpec((None, rows, HEAD_DIM), lambda b, kv, i: (kv, 0, 0)),
        ],
        out_specs=pl.BlockSpec((group, None, tq, HEAD_DIM), lambda b, kv, i: (kv, b, i, 0)),
        scratch_shapes=[
            pltpu.VMEM((seq, 2 * HEAD_DIM), BF16),
            pltpu.VMEM((rows, span), F32),
            pltpu.VMEM((rows, span), F32),
        ],
        compiler_params=_params(("parallel", "parallel", "arbitrary")),
        name="sw",
    )(qv, kt, qv, bias, sink)


def _merge_kernel(x_ref, g_ref, oa_ref, ob_ref, oc_ref, wg_ref, wb_ref, wo_ref, o_ref, h_ref, merged_ref):
    j = pl.program_id(1)

    @pl.when(j == 0)
    def _():
        h_ref[...] = _rms(x_ref[...], g_ref[...]).astype(BF16)

    h = h_ref[...]
    merged = None
    for n, br_ref in enumerate((oa_ref, ob_ref, oc_ref)):
        branch = jnp.concatenate([br_ref[hd] for hd in range(br_ref.shape[0])], axis=1)
        term = jax.nn.sigmoid(_dot(h, wg_ref[n])) * _dot(branch, wb_ref[n])
        merged = term if merged is None else merged + term
    merged_ref[j] = merged.astype(BF16)

    @pl.when(j == pl.num_programs(1) - 1)
    def _():
        merged_all = jnp.concatenate([merged_ref[c] for c in range(merged_ref.shape[0])], axis=1)
        o_ref[...] = x_ref[...] + _dot(merged_all, wo_ref[...])


def _merge(x, g, oa, ob, oc, w_gate, w_branch, w_out):
    t, d = x.shape
    nh = oa.shape[0]
    bw = nh * HEAD_DIM
    tm = _pick_tile(t, 512)
    tn = _pick_tile(d, 256)
    return pl.pallas_call(
        _merge_kernel,
        out_shape=jax.ShapeDtypeStruct((t, d), F32),
        grid=(t // tm, d // tn),
        in_specs=[
            pl.BlockSpec((tm, d), lambda i, j: (i, 0)),
            pl.BlockSpec((1, d), lambda i, j: (0, 0)),
            pl.BlockSpec((nh, tm, HEAD_DIM), lambda i, j: (0, i, 0)),
            pl.BlockSpec((nh, tm, HEAD_DIM), lambda i, j: (0, i, 0)),
            pl.BlockSpec((nh, tm, HEAD_DIM), lambda i, j: (0, i, 0)),
            pl.BlockSpec((3, d, tn), lambda i, j: (0, 0, j)),
            pl.BlockSpec((3, bw, tn), lambda i, j: (0, 0, j)),
            pl.BlockSpec((d, d), lambda i, j: (0, 0), pipeline_mode=pl.Buffered(1)),
        ],
        out_specs=pl.BlockSpec((tm, d), lambda i, j: (i, 0)),
        scratch_shapes=[pltpu.VMEM((tm, d), BF16), pltpu.VMEM((d // tn, tm, tn), BF16)],
        compiler_params=_params(("parallel", "arbitrary")),
        name="merge",
    )(x, g, oa, ob, oc, w_gate, w_branch, w_out)


def _rope_tables(seq):
    quarter = HEAD_DIM // 4
    freqs = ROPE_THETA ** (-jnp.arange(quarter, dtype=F32) / quarter)
    t = jnp.arange(seq)
    row = (t // GRID_W).astype(F32)
    col = (t % GRID_W).astype(F32)
    ang_r = row[:, None] * freqs[None, :]
    ang_c = col[:, None] * freqs[None, :]
    cos = jnp.concatenate([jnp.cos(ang_r)] * 2 + [jnp.cos(ang_c)] * 2, axis=-1)
    sin = jnp.concatenate([-jnp.sin(ang_r), jnp.sin(ang_r), -jnp.sin(ang_c), jnp.sin(ang_c)], axis=-1)
    return cos, sin


def _gain_row(na_q, na_k, ga_q, ga_k, sw_q, sw_k):
    scale = HEAD_DIM ** -0.5 * LOG2E
    one = jnp.ones((HEAD_DIM,), F32)
    parts = ([na_q * scale] * 8 + [na_k] * 8 + [one] * 8 + [ga_q * scale] * 8 + [ga_k] * 2 + [one] * 2
             + [sw_q * scale] * 8 + [sw_k] * 2 + [one] * 2)
    return jnp.concatenate([p.astype(F32) for p in parts]).reshape(1, D_IN)


def _ga_shift(ga_q, ga_k):
    fold = HEAD_DIM ** -0.5 * LOG2E
    bound = (HEAD_DIM * fold * 1.02 * jnp.max(jnp.abs(ga_q.astype(F32)))
             * jnp.max(jnp.abs(ga_k.astype(F32))))
    return jnp.zeros((1,), F32), 2.0 * bound <= MAX_EXP2_RANGE


def _sw_sink_rows(sink, group):
    per_head = (sink.astype(F32) * LOG2E).reshape(SW_KV_HEADS, group, 1, 1)
    return jnp.broadcast_to(per_head, (SW_KV_HEADS, group, Q_BLOCK, HEAD_DIM)).reshape(
        SW_KV_HEADS, group * Q_BLOCK, HEAD_DIM)


def _trunk(x3, layers, rope):
    batch, seq, d = x3.shape
    x = x3.reshape(batch * seq, d)
    cos, sin = rope
    for p in layers:
        x = _ffn(x, p["ffn1_norm"], p["ffn1_wg"], p["ffn1_wu"], p["ffn1_wd"])
        qv, kt = _qkv(x, p["mix_norm"], p["w_in"], p["gain"], cos, sin, seq)
        qv = qv.reshape(N_QV, batch, seq, HEAD_DIM)
        shift, shift_ok = p["ga_shift"]
        oa = _na(qv, kt, p["na_bias"], batch, seq)
        ob = lax.cond(shift_ok,
                      lambda: _ga(shift, qv, kt, batch, seq, True),
                      lambda: _ga(shift, qv, kt, batch, seq, False))
        oc = _sw(qv, kt, p["sw_bias"], p["sw_sink"], batch, seq)
        oa = oa.reshape(NA_HEADS, batch * seq, HEAD_DIM)
        ob = ob.reshape(GA_HEADS, batch * seq, HEAD_DIM)
        oc = oc.reshape(SW_HEADS, batch * seq, HEAD_DIM)
        x = _merge(x, p["mix_norm"], oa, ob, oc, p["w_gate"], p["w_branch"], p["w_out"])
        x = _ffn(x, p["ffn2_norm"], p["ffn2_wg"], p["ffn2_wu"], p["ffn2_wd"])
    return x.reshape(batch, seq, d)


def kernel(x_prompt, x_sample, ffn1_norm, ffn1_wg, ffn1_wu, ffn1_wd, mix_norm, w_in, w_gate, na_q_norm, na_k_norm, na_rel_bias, ga_q_norm, ga_k_norm, sw_q_norm, sw_k_norm, sw_sink, w_branch, w_out, ffn2_norm, ffn2_wg, ffn2_wu, ffn2_wd):
    depth = w_in.shape[0]
    d = x_prompt.shape[-1]
    sw_group = SW_HEADS // SW_KV_HEADS
    sw_bias = _sw_bias_table()
    layers = []
    for l in range(depth):
        layers.append(dict(
            ffn1_norm=ffn1_norm[l].reshape(1, d).astype(F32),
            ffn1_wg=ffn1_wg[l].astype(BF16), ffn1_wu=ffn1_wu[l].astype(BF16), ffn1_wd=ffn1_wd[l].astype(BF16),
            mix_norm=mix_norm[l].reshape(1, d).astype(F32),
            w_in=w_in[l].astype(BF16),
            gain=_gain_row(na_q_norm[l], na_k_norm[l], ga_q_norm[l], ga_k_norm[l], sw_q_norm[l], sw_k_norm[l]),
            na_bias=_na_bias_table(na_rel_bias[l]),
            ga_shift=_ga_shift(ga_q_norm[l], ga_k_norm[l]),
            sw_bias=sw_bias,
            sw_sink=_sw_sink_rows(sw_sink[l], sw_group),
            w_gate=w_gate[l].astype(BF16), w_branch=w_branch[l].astype(BF16), w_out=w_out[l].astype(BF16),
            ffn2_norm=ffn2_norm[l].reshape(1, d).astype(F32),
            ffn2_wg=ffn2_wg[l].astype(BF16), ffn2_wu=ffn2_wu[l].astype(BF16), ffn2_wd=ffn2_wd[l].astype(BF16),
        ))
    outs = []
    for x3 in (x_prompt, x_sample):
        outs.append(_trunk(x3, layers, _rope_tables(x3.shape[1])))
    return tuple(outs)
```

```python
import functools
import math

import numpy as np
import jax
import jax.numpy as jnp
from jax import lax
from jax.experimental import pallas as pl
from jax.experimental.pallas import tpu as pltpu

F32 = jnp.float32
BF16 = jnp.bfloat16

EPS = 1e-6
HEAD_DIM = 128
GRID_W = 64
NA_HEADS = 8
NA_ROWS = 8
NA_COLS = 16
GA_HEADS = 8
GA_KV_HEADS = 2
SW_HEADS = 8
SW_KV_HEADS = 2
WINDOW = 128
Q_BLOCK = 128
ROPE_THETA = 10000.0
MASKED = -1e30
LOG2E = math.log2(math.e)
MAX_EXP2_RANGE = 100.0

N_IN_HEADS = 48
D_IN = N_IN_HEADS * HEAD_DIM

NA_Q0, GA_Q0, SW_Q0, NA_V0, GA_V0, SW_V0 = 0, 8, 16, 24, 32, 34
N_QV = 36
NA_K0, GA_K0, SW_K0 = 0, 8, 10
N_KT = 12

QKV_SEGMENTS = (
    (0, 8, True, False, False, NA_Q0), (8, 8, True, False, True, NA_K0), (16, 8, False, False, False, NA_V0),
    (24, 8, True, True, False, GA_Q0), (32, 2, True, True, True, GA_K0), (34, 2, False, False, False, GA_V0),
    (36, 8, True, False, False, SW_Q0), (44, 2, True, False, True, SW_K0), (46, 2, False, False, False, SW_V0),
)

NA_QROWS = 8
NA_KROWS = 16
NA_TQ = NA_QROWS * GRID_W
NA_TK = NA_KROWS * GRID_W

VMEM_LIMIT_BYTES = 56 * 1024 * 1024
FFN_TF = 512
MERGE_TN = 256


def _params(sem):
    return pltpu.CompilerParams(dimension_semantics=sem, vmem_limit_bytes=VMEM_LIMIT_BYTES)


def _rms(xf, g):
    ms = jnp.mean(xf * xf, axis=-1, keepdims=True)
    return xf * lax.rsqrt(ms + EPS) * g


def _dot(a, b):
    return jnp.dot(a, b, preferred_element_type=F32)


def _pick_tile(n, want):
    t = min(n, want)
    while n % t:
        t //= 2
    return t


def _ffn_kernel(x_ref, g_ref, wga_ref, wua_ref, wda_ref, wgb_ref, wub_ref, wdb_ref, o_ref, h_ref,
                *, first_single):
    j = pl.program_id(1)

    def down(wg_ref, wu_ref, wd_ref):
        h = h_ref[...]
        a = _dot(h, wg_ref[...])
        u = _dot(h, wu_ref[...])
        act = (a * jax.nn.sigmoid(a) * u).astype(BF16)
        return _dot(act, wd_ref[...])

    @pl.when(j == 0)
    def _():
        h_ref[...] = _rms(x_ref[...], g_ref[...]).astype(BF16)
        o_ref[...] = down(wgb_ref, wub_ref, wdb_ref)
        if not first_single:
            o_ref[...] += down(wga_ref, wua_ref, wda_ref)

    @pl.when(j > 0)
    def _():
        o_ref[...] += down(wga_ref, wua_ref, wda_ref)
        o_ref[...] += down(wgb_ref, wub_ref, wdb_ref)

    @pl.when(j == pl.num_programs(1) - 1)
    def _():
        o_ref[...] = x_ref[...] + 0.5 * o_ref[...]


def _col_tiles(w, want):
    n = w.shape[-1]
    tile = _pick_tile(n, want)
    w = w.astype(BF16).reshape(w.shape[:-1] + (n // tile, tile))
    return jnp.moveaxis(w, -2, 0)


def _ffn(x, g, wg, wu, wd):
    t, d = x.shape
    nf, _, tf = wg.shape
    tm = _pick_tile(t, 512)
    first_single = nf % 2

    def a_idx(j):
        return jnp.maximum(2 * j - first_single, 0)

    def b_idx(j):
        return 2 * j + 1 - first_single

    return pl.pallas_call(
        functools.partial(_ffn_kernel, first_single=bool(first_single)),
        out_shape=jax.ShapeDtypeStruct((t, d), F32),
        grid=(t // tm, nf // 2 + first_single),
        in_specs=[
            pl.BlockSpec((tm, d), lambda i, j: (i, 0)),
            pl.BlockSpec((1, d), lambda i, j: (0, 0)),
            pl.BlockSpec((None, d, tf), lambda i, j: (a_idx(j), 0, 0)),
            pl.BlockSpec((None, d, tf), lambda i, j: (a_idx(j), 0, 0)),
            pl.BlockSpec((tf, d), lambda i, j: (a_idx(j), 0)),
            pl.BlockSpec((None, d, tf), lambda i, j: (b_idx(j), 0, 0)),
            pl.BlockSpec((None, d, tf), lambda i, j: (b_idx(j), 0, 0)),
            pl.BlockSpec((tf, d), lambda i, j: (b_idx(j), 0)),
        ],
        out_specs=pl.BlockSpec((tm, d), lambda i, j: (i, 0)),
        scratch_shapes=[pltpu.VMEM((tm, d), BF16)],
        compiler_params=_params(("parallel", "arbitrary")),
        name="ffn",
    )(x, g, wg, wu, wd, wg, wu, wd)


def _swap_rotary_halves(y):
    lane = lax.broadcasted_iota(jnp.int32, y.shape, 1)
    first = (lane & 32) == 0
    return jnp.where(first, pltpu.roll(y, 96, 1), pltpu.roll(y, 32, 1))


def _qkv_kernel(x_ref, g_ref, w_ref, gain_ref, cos_ref, sin_ref, qv_ref, kt_ref):
    h = _rms(x_ref[...], g_ref[...]).astype(BF16)
    for h0, nh, norm, rope, is_key, slot0 in QKV_SEGMENTS:
        c0 = h0 * HEAD_DIM
        y = _dot(h, w_ref[:, c0:c0 + nh * HEAD_DIM])
        for hh in range(nh):
            lo = hh * HEAD_DIM
            yh = y[:, lo:lo + HEAD_DIM]
            if norm:
                yh = _rms(yh, gain_ref[:, c0 + lo:c0 + lo + HEAD_DIM])
            if rope:
                yh = yh * cos_ref[...] + _swap_rotary_halves(yh) * sin_ref[...]
            if is_key:
                kt_ref[slot0 + hh] = yh.T.astype(BF16)
            else:
                qv_ref[slot0 + hh] = yh.astype(BF16)


def _qkv(x, g, w_in, gain, cos, sin, seq):
    t, d = x.shape
    tm = _pick_tile(seq, 256)
    nseq = seq // tm
    return pl.pallas_call(
        _qkv_kernel,
        out_shape=(jax.ShapeDtypeStruct((N_QV, t, HEAD_DIM), BF16),
                   jax.ShapeDtypeStruct((N_KT, HEAD_DIM, t), BF16)),
        grid=(t // tm,),
        in_specs=[
            pl.BlockSpec((tm, d), lambda i: (i, 0)),
            pl.BlockSpec((1, d), lambda i: (0, 0)),
            pl.BlockSpec((d, D_IN), lambda i: (0, 0), pipeline_mode=pl.Buffered(1)),
            pl.BlockSpec((1, D_IN), lambda i: (0, 0)),
            pl.BlockSpec((tm, HEAD_DIM), lambda i: (i % nseq, 0)),
            pl.BlockSpec((tm, HEAD_DIM), lambda i: (i % nseq, 0)),
        ],
        out_specs=(pl.BlockSpec((N_QV, tm, HEAD_DIM), lambda i: (0, i, 0)),
                   pl.BlockSpec((N_KT, HEAD_DIM, tm), lambda i: (0, 0, i))),
        compiler_params=_params(("parallel",)),
        name="qkv",
    )(x, g, w_in, gain, cos, sin)


def _edge_variant(i, nblk):
    return jnp.where(i == 0, 0, jnp.where(i == nblk - 1, 2, 1))


def _two_stage_loop(nblk, logits, finish, sa_ref, sb_ref):
    logits(0, sa_ref)

    def body(j, carry):
        i = 2 * j
        logits(i + 1, sb_ref)
        finish(i, sa_ref)
        logits(i + 2, sa_ref)
        finish(i + 1, sb_ref)
        return carry

    lax.fori_loop(0, nblk // 2 - 1, body, 0)
    logits(nblk - 1, sb_ref)
    finish(nblk - 2, sa_ref)
    finish(nblk - 1, sb_ref)


def _na_kernel(q_ref, kt_ref, v_ref, bias_ref, o_ref, v1_ref, sa_ref, sb_ref, *, rows):
    nblk = q_ref.shape[0] // NA_TQ
    v1_ref[:, :HEAD_DIM] = v_ref[...]
    v1_ref[:, HEAD_DIM:] = jnp.ones(v_ref.shape, BF16)

    def window(i):
        start_row = jnp.clip(i * NA_QROWS - NA_ROWS // 2, 0, rows - NA_KROWS)
        return pl.multiple_of(start_row * GRID_W, (NA_ROWS // 2) * GRID_W)

    def logits(i, dst_ref):
        q = q_ref[pl.ds(pl.multiple_of(i * NA_TQ, NA_TQ), NA_TQ), :]
        dst_ref[...] = _dot(q, kt_ref[:, pl.ds(window(i), NA_TK)])

    def finish(i, src_ref):
        s = src_ref[...] + bias_ref[_edge_variant(i, nblk)]
        m = jnp.max(s, axis=-1, keepdims=True)
        p = jnp.exp2(s - m).astype(BF16)
        pv = _dot(p, v1_ref[pl.ds(window(i), NA_TK), :])
        o = pv[:, :HEAD_DIM] / pv[:, HEAD_DIM:]
        o_ref[pl.ds(pl.multiple_of(i * NA_TQ, NA_TQ), NA_TQ), :] = o.astype(BF16)

    _two_stage_loop(nblk, logits, finish, sa_ref, sb_ref)


def _na(qv, kt, bias, batch, seq):
    rows = seq // GRID_W
    assert seq % (2 * NA_TQ) == 0 and rows >= NA_KROWS
    return pl.pallas_call(
        functools.partial(_na_kernel, rows=rows),
        out_shape=jax.ShapeDtypeStruct((NA_HEADS, batch, seq, HEAD_DIM), BF16),
        grid=(NA_HEADS, batch),
        in_specs=[
            pl.BlockSpec((None, None, seq, HEAD_DIM), lambda h, b: (NA_Q0 + h, b, 0, 0)),
            pl.BlockSpec((None, HEAD_DIM, seq), lambda h, b: (NA_K0 + h, 0, b)),
            pl.BlockSpec((None, None, seq, HEAD_DIM), lambda h, b: (NA_V0 + h, b, 0, 0)),
            pl.BlockSpec((None, 3, NA_TQ, NA_TK), lambda h, b: (h, 0, 0, 0)),
        ],
        out_specs=pl.BlockSpec((None, None, seq, HEAD_DIM), lambda h, b: (h, b, 0, 0)),
        scratch_shapes=[
            pltpu.VMEM((seq, 2 * HEAD_DIM), BF16),
            pltpu.VMEM((NA_TQ, NA_TK), F32),
            pltpu.VMEM((NA_TQ, NA_TK), F32),
        ],
        compiler_params=_params(("parallel", "parallel")),
        name="na",
    )(qv, kt, qv, bias)


def _na_bias_table(rel_bias):
    rel = rel_bias.astype(F32) * LOG2E
    rel_pad = jnp.pad(rel, ((0, 0), (0, 0), (GRID_W, GRID_W)))
    off = GRID_W + NA_COLS - 1
    t1 = jnp.stack([rel_pad[:, :, off - c:off - c + GRID_W] for c in range(GRID_W)], axis=1)
    cq = np.arange(GRID_W)[:, None]
    kc = np.arange(GRID_W)[None, :]
    cs = np.clip(cq - NA_COLS // 2, 0, GRID_W - NA_COLS)
    col_ok = (kc >= cs) & (kc < cs + NA_COLS)
    t1 = jnp.where(col_ok[None, :, None, :], t1, MASKED)

    rows = 3 * NA_QROWS
    blocks = []
    for v in range(3):
        start_row = int(np.clip(v * NA_QROWS - NA_ROWS // 2, 0, rows - NA_KROWS))
        for qr in range(NA_QROWS):
            r = v * NA_QROWS + qr
            rs = int(np.clip(r - NA_ROWS // 2, 0, rows - NA_ROWS))
            lo = rs - start_row
            dr0 = rs - r + NA_ROWS - 1
            blk = t1[:, :, dr0:dr0 + NA_ROWS, :]
            blocks.append(jnp.pad(blk, ((0, 0), (0, 0), (lo, NA_KROWS - NA_ROWS - lo), (0, 0)),
                                  constant_values=MASKED))
    table = jnp.stack(blocks, axis=1)
    return table.reshape(NA_HEADS, 3, NA_TQ, NA_TK)


def _ga_kernel(bound_ref, q_ref, kt_ref, v_ref, o_ref, v1_ref, m_ref, acc_ref, sa_ref, sb_ref,
               *, tq, tk, group, bounded):
    @pl.when(pl.program_id(2) == 0)
    def _():
        v1_ref[:, :HEAD_DIM] = v_ref[...]
        v1_ref[:, HEAD_DIM:] = jnp.ones(v_ref.shape, BF16)

    m_ref[...] = jnp.full(m_ref.shape, MASKED, F32)
    acc_ref[...] = jnp.zeros(acc_ref.shape, F32)
    lane_tiles = tk // HEAD_DIM
    n_chunks = kt_ref.shape[1] // tk

    def logits(c, dst_ref):
        start = pl.multiple_of(c * tk, tk)
        q = q_ref[...].reshape(group * tq, HEAD_DIM)
        s = _dot(q, kt_ref[:, pl.ds(start, tk)])
        if bounded:
            dst_ref[...] = jnp.exp2(s - bound_ref[0]).astype(BF16)
        else:
            dst_ref[...] = s

    def accumulate(c, src_ref):
        start = pl.multiple_of(c * tk, tk)
        if bounded:
            acc_ref[...] += _dot(src_ref[...], v1_ref[pl.ds(start, tk), :])
            return
        s = src_ref[...]
        m_prev = m_ref[...]
        m_new = jnp.maximum(m_prev, jnp.max(s, axis=-1, keepdims=True))
        alpha = jnp.exp2(m_prev - m_new)
        p = jnp.exp2(s - jnp.concatenate([m_new] * lane_tiles, axis=1))
        pv = _dot(p.astype(BF16), v1_ref[pl.ds(start, tk), :])
        acc_ref[...] = jnp.concatenate([alpha, alpha], axis=1) * acc_ref[...] + pv
        m_ref[...] = m_new

    _two_stage_loop(n_chunks, logits, accumulate, sa_ref, sb_ref)
    o = acc_ref[:, :HEAD_DIM] / acc_ref[:, HEAD_DIM:]
    o_ref[...] = o.reshape(group, tq, HEAD_DIM).astype(BF16)


def _ga(bound, qv, kt, batch, seq, bounded):
    group = GA_HEADS // GA_KV_HEADS
    tq = _pick_tile(seq, 256)
    tk = _pick_tile(seq, 512)
    assert (seq // tk) % 2 == 0
    score_dtype = BF16 if bounded else F32
    return pl.pallas_call(
        functools.partial(_ga_kernel, tq=tq, tk=tk, group=group, bounded=bounded),
        out_shape=jax.ShapeDtypeStruct((GA_HEADS, batch, seq, HEAD_DIM), BF16),
        grid=(batch, GA_KV_HEADS, seq // tq),
        in_specs=[
            pl.BlockSpec(memory_space=pltpu.SMEM),
            pl.BlockSpec((group, None, tq, HEAD_DIM), lambda b, kv, i: (GA_Q0 // group + kv, b, i, 0)),
            pl.BlockSpec((None, HEAD_DIM, seq), lambda b, kv, i: (GA_K0 + kv, 0, b)),
            pl.BlockSpec((None, None, seq, HEAD_DIM), lambda b, kv, i: (GA_V0 + kv, b, 0, 0)),
        ],
        out_specs=pl.BlockSpec((group, None, tq, HEAD_DIM), lambda b, kv, i: (kv, b, i, 0)),
        scratch_shapes=[
            pltpu.VMEM((seq, 2 * HEAD_DIM), BF16),
            pltpu.VMEM((group * tq, HEAD_DIM), F32),
            pltpu.VMEM((group * tq, 2 * HEAD_DIM), F32),
            pltpu.VMEM((group * tq, tk), score_dtype),
            pltpu.VMEM((group * tq, tk), score_dtype),
        ],
        compiler_params=_params(("parallel", "parallel", "arbitrary")),
        name="ga_bounded" if bounded else "ga",
    )(bound, qv, kt, qv)


def _sw_kernel(q_ref, kt_ref, v_ref, bias_ref, sink_ref, o_ref, v1_ref, sa_ref, sb_ref, *, seq, group):
    span = Q_BLOCK + 2 * WINDOW
    nblk_seq = seq // Q_BLOCK
    nblk = q_ref.shape[1] // Q_BLOCK
    first = pl.program_id(2) * nblk

    @pl.when(pl.program_id(2) == 0)
    def _():
        v1_ref[:, :HEAD_DIM] = v_ref[...]
        v1_ref[:, HEAD_DIM:] = jnp.ones(v_ref.shape, BF16)

    def window(n):
        return pl.multiple_of(jnp.clip((first + n) * Q_BLOCK - WINDOW, 0, seq - span), Q_BLOCK)

    def rows(n):
        return pl.ds(pl.multiple_of(n * Q_BLOCK, Q_BLOCK), Q_BLOCK)

    def logits(n, dst_ref):
        q = q_ref[:, rows(n), :].reshape(group * Q_BLOCK, HEAD_DIM)
        dst_ref[...] = _dot(q, kt_ref[:, pl.ds(window(n), span)])

    def finish(n, src_ref):
        sink = sink_ref[...]
        s = src_ref[...] + bias_ref[_edge_variant(first + n, nblk_seq)]
        m = jnp.maximum(jnp.max(s, axis=-1, keepdims=True), sink)
        p = jnp.exp2(s - jnp.concatenate([m] * (span // HEAD_DIM), axis=1)).astype(BF16)
        pv = _dot(p, v1_ref[pl.ds(window(n), span), :])
        o = pv[:, :HEAD_DIM] / (pv[:, HEAD_DIM:] + jnp.exp2(sink - m))
        o_ref[:, rows(n), :] = o.reshape(group, Q_BLOCK, HEAD_DIM).astype(BF16)

    _two_stage_loop(nblk, logits, finish, sa_ref, sb_ref)


def _sw_bias_table():
    group = SW_HEADS // SW_KV_HEADS
    span = Q_BLOCK + 2 * WINDOW
    slopes = 2.0 ** (-8.0 * np.arange(1, SW_HEADS + 1, dtype=np.float64) / SW_HEADS)
    qi = np.arange(Q_BLOCK)[:, None]
    kj = np.arange(span)[None, :]
    table = np.empty((SW_KV_HEADS, 3, group, Q_BLOCK, span), np.float32)
    for v in range(3):
        dist = np.abs(qi - (kj - v * WINDOW))
        for h in range(SW_HEADS):
            table[h // group, v, h % group] = np.where(dist <= WINDOW, -LOG2E * slopes[h] * dist, MASKED)
    return jnp.asarray(table.reshape(SW_KV_HEADS, 3, group * Q_BLOCK, span))


def _sw(qv, kt, bias, sink, batch, seq):
    group = SW_HEADS // SW_KV_HEADS
    span = Q_BLOCK + 2 * WINDOW
    tq = _pick_tile(seq, 16 * Q_BLOCK)
    assert tq % (2 * Q_BLOCK) == 0 and seq >= span
    rows = group * Q_BLOCK
    return pl.pallas_call(
        functools.partial(_sw_kernel, seq=seq, group=group),
        out_shape=jax.ShapeDtypeStruct((SW_HEADS, batch, seq, HEAD_DIM), BF16),
        grid=(batch, SW_KV_HEADS, seq // tq),
        in_specs=[
            pl.BlockSpec((group, None, tq, HEAD_DIM), lambda b, kv, i: (SW_Q0 // group + kv, b, i, 0)),
            pl.BlockSpec((None, HEAD_DIM, seq), lambda b, kv, i: (SW_K0 + kv, 0, b)),
            pl.BlockSpec((None, None, seq, HEAD_DIM), lambda b, kv, i: (SW_V0 + kv, b, 0, 0)),
            pl.BlockSpec((None, 3, rows, span), lambda b, kv, i: (kv, 0, 0, 0)),
            pl.BlockSpec((None, rows, HEAD_DIM), lambda b, kv, i: (kv, 0, 0)),
        ],
        out_specs=pl.BlockSpec((group, None, tq, HEAD_DIM), lambda b, kv, i: (kv, b, i, 0)),
        scratch_shapes=[
            pltpu.VMEM((seq, 2 * HEAD_DIM), BF16),
            pltpu.VMEM((rows, span), F32),
            pltpu.VMEM((rows, span), F32),
        ],
        compiler_params=_params(("parallel", "parallel", "arbitrary")),
        name="sw",
    )(qv, kt, qv, bias, sink)


def _merge_kernel(x_ref, g_ref, oa_ref, ob_ref, oc_ref, wg_ref, wb_ref, wo_ref, o_ref, h_ref, merged_ref):
    j = pl.program_id(1)

    @pl.when(j == 0)
    def _():
        h_ref[...] = _rms(x_ref[...], g_ref[...]).astype(BF16)

    h = h_ref[...]
    merged = None
    for n, br_ref in enumerate((oa_ref, ob_ref, oc_ref)):
        branch = jnp.concatenate([br_ref[hd] for hd in range(br_ref.shape[0])], axis=1)
        term = jax.nn.sigmoid(_dot(h, wg_ref[n])) * _dot(branch, wb_ref[n])
        merged = term if merged is None else merged + term
    merged_ref[j] = merged.astype(BF16)

    @pl.when(j == pl.num_programs(1) - 1)
    def _():
        merged_all = jnp.concatenate([merged_ref[c] for c in range(merged_ref.shape[0])], axis=1)
        o_ref[...] = x_ref[...] + _dot(merged_all, wo_ref[...])


def _merge(x, g, oa, ob, oc, w_gate, w_branch, w_out):
    t, d = x.shape
    nh = oa.shape[0]
    bw = nh * HEAD_DIM
    tm = _pick_tile(t, 512)
    tn = w_gate.shape[-1]
    return pl.pallas_call(
        _merge_kernel,
        out_shape=jax.ShapeDtypeStruct((t, d), F32),
        grid=(t // tm, d // tn),
        in_specs=[
            pl.BlockSpec((tm, d), lambda i, j: (i, 0)),
            pl.BlockSpec((1, d), lambda i, j: (0, 0)),
            pl.BlockSpec((nh, tm, HEAD_DIM), lambda i, j: (0, i, 0)),
            pl.BlockSpec((nh, tm, HEAD_DIM), lambda i, j: (0, i, 0)),
            pl.BlockSpec((nh, tm, HEAD_DIM), lambda i, j: (0, i, 0)),
            pl.BlockSpec((None, 3, d, tn), lambda i, j: (j, 0, 0, 0)),
            pl.BlockSpec((None, 3, bw, tn), lambda i, j: (j, 0, 0, 0)),
            pl.BlockSpec((d, d), lambda i, j: (0, 0), pipeline_mode=pl.Buffered(1)),
        ],
        out_specs=pl.BlockSpec((tm, d), lambda i, j: (i, 0)),
        scratch_shapes=[pltpu.VMEM((tm, d), BF16), pltpu.VMEM((d // tn, tm, tn), BF16)],
        compiler_params=_params(("parallel", "arbitrary")),
        name="merge",
    )(x, g, oa, ob, oc, w_gate, w_branch, w_out)


def _rope_tables(seq):
    quarter = HEAD_DIM // 4
    freqs = ROPE_THETA ** (-jnp.arange(quarter, dtype=F32) / quarter)
    t = jnp.arange(seq)
    row = (t // GRID_W).astype(F32)
    col = (t % GRID_W).astype(F32)
    ang_r = row[:, None] * freqs[None, :]
    ang_c = col[:, None] * freqs[None, :]
    cos = jnp.concatenate([jnp.cos(ang_r)] * 2 + [jnp.cos(ang_c)] * 2, axis=-1)
    sin = jnp.concatenate([-jnp.sin(ang_r), jnp.sin(ang_r), -jnp.sin(ang_c), jnp.sin(ang_c)], axis=-1)
    return cos, sin


def _gain_row(na_q, na_k, ga_q, ga_k, sw_q, sw_k):
    scale = HEAD_DIM ** -0.5 * LOG2E
    one = jnp.ones((HEAD_DIM,), F32)
    parts = ([na_q * scale] * 8 + [na_k] * 8 + [one] * 8 + [ga_q * scale] * 8 + [ga_k] * 2 + [one] * 2
             + [sw_q * scale] * 8 + [sw_k] * 2 + [one] * 2)
    return jnp.concatenate([p.astype(F32) for p in parts]).reshape(1, D_IN)


def _ga_shift(ga_q, ga_k):
    fold = HEAD_DIM ** -0.5 * LOG2E
    bound = (HEAD_DIM * fold * 1.02 * jnp.max(jnp.abs(ga_q.astype(F32)))
             * jnp.max(jnp.abs(ga_k.astype(F32))))
    return jnp.zeros((1,), F32), 2.0 * bound <= MAX_EXP2_RANGE


def _sw_sink_rows(sink, group):
    per_head = (sink.astype(F32) * LOG2E).reshape(SW_KV_HEADS, group, 1, 1)
    return jnp.broadcast_to(per_head, (SW_KV_HEADS, group, Q_BLOCK, HEAD_DIM)).reshape(
        SW_KV_HEADS, group * Q_BLOCK, HEAD_DIM)


def _trunk(x3, layers, rope):
    batch, seq, d = x3.shape
    x = x3.reshape(batch * seq, d)
    cos, sin = rope
    for p in layers:
        x = _ffn(x, p["ffn1_norm"], p["ffn1_wg"], p["ffn1_wu"], p["ffn1_wd"])
        qv, kt = _qkv(x, p["mix_norm"], p["w_in"], p["gain"], cos, sin, seq)
        qv = qv.reshape(N_QV, batch, seq, HEAD_DIM)
        shift, shift_ok = p["ga_shift"]
        oa = _na(qv, kt, p["na_bias"], batch, seq)
        ob = lax.cond(shift_ok,
                      lambda: _ga(shift, qv, kt, batch, seq, True),
                      lambda: _ga(shift, qv, kt, batch, seq, False))
        oc = _sw(qv, kt, p["sw_bias"], p["sw_sink"], batch, seq)
        oa = oa.reshape(NA_HEADS, batch * seq, HEAD_DIM)
        ob = ob.reshape(GA_HEADS, batch * seq, HEAD_DIM)
        oc = oc.reshape(SW_HEADS, batch * seq, HEAD_DIM)
        x = _merge(x, p["mix_norm"], oa, ob, oc, p["w_gate"], p["w_branch"], p["w_out"])
        x = _ffn(x, p["ffn2_norm"], p["ffn2_wg"], p["ffn2_wu"], p["ffn2_wd"])
    return x.reshape(batch, seq, d)


def kernel(x_prompt, x_sample, ffn1_norm, ffn1_wg, ffn1_wu, ffn1_wd, mix_norm, w_in, w_gate, na_q_norm, na_k_norm, na_rel_bias, ga_q_norm, ga_k_norm, sw_q_norm, sw_k_norm, sw_sink, w_branch, w_out, ffn2_norm, ffn2_wg, ffn2_wu, ffn2_wd):
    depth = w_in.shape[0]
    d = x_prompt.shape[-1]
    sw_group = SW_HEADS // SW_KV_HEADS
    sw_bias = _sw_bias_table()
    layers = []
    for l in range(depth):
        layers.append(dict(
            ffn1_norm=ffn1_norm[l].reshape(1, d).astype(F32),
            ffn1_wg=_col_tiles(ffn1_wg[l], FFN_TF), ffn1_wu=_col_tiles(ffn1_wu[l], FFN_TF),
            ffn1_wd=ffn1_wd[l].astype(BF16),
            mix_norm=mix_norm[l].reshape(1, d).astype(F32),
            w_in=w_in[l].astype(BF16),
            gain=_gain_row(na_q_norm[l], na_k_norm[l], ga_q_norm[l], ga_k_norm[l], sw_q_norm[l], sw_k_norm[l]),
            na_bias=_na_bias_table(na_rel_bias[l]),
            ga_shift=_ga_shift(ga_q_norm[l], ga_k_norm[l]),
            sw_bias=sw_bias,
            sw_sink=_sw_sink_rows(sw_sink[l], sw_group),
            w_gate=_col_tiles(w_gate[l], MERGE_TN), w_branch=_col_tiles(w_branch[l], MERGE_TN),
            w_out=w_out[l].astype(BF16),
            ffn2_norm=ffn2_norm[l].reshape(1, d).astype(F32),
            ffn2_wg=_col_tiles(ffn2_wg[l], FFN_TF), ffn2_wu=_col_tiles(ffn2_wu[l], FFN_TF),
            ffn2_wd=ffn2_wd[l].astype(BF16),
        ))
    outs = []
    for x3 in (x_prompt, x_sample):
        outs.append(_trunk(x3, layers, _rope_tables(x3.shape[1])))
    return tuple(outs)
```

```python
import functools
import math

import numpy as np
import jax
import jax.numpy as jnp
from jax import lax
from jax.experimental import pallas as pl
from jax.experimental.pallas import tpu as pltpu

F32 = jnp.float32
BF16 = jnp.bfloat16

EPS = 1e-6
HEAD_DIM = 128
GRID_W = 64
NA_HEADS = 8
NA_ROWS = 8
NA_COLS = 16
GA_HEADS = 8
GA_KV_HEADS = 2
SW_HEADS = 8
SW_KV_HEADS = 2
WINDOW = 128
Q_BLOCK = 128
ROPE_THETA = 10000.0
MASKED = -1e30
LOG2E = math.log2(math.e)
MAX_EXP2_RANGE = 100.0

N_IN_HEADS = 48
D_IN = N_IN_HEADS * HEAD_DIM

NA_Q0, GA_Q0, SW_Q0, NA_V0, GA_V0, SW_V0 = 0, 8, 16, 24, 32, 34
N_QV = 36
NA_K0, GA_K0, SW_K0 = 0, 8, 10
N_KT = 12

QKV_SEGMENTS = (
    (0, 8, True, False, False, NA_Q0), (8, 8, True, False, True, NA_K0), (16, 8, False, False, False, NA_V0),
    (24, 8, True, True, False, GA_Q0), (32, 2, True, True, True, GA_K0), (34, 2, False, False, False, GA_V0),
    (36, 8, True, False, False, SW_Q0), (44, 2, True, False, True, SW_K0), (46, 2, False, False, False, SW_V0),
)

NA_QROWS = 8
NA_KROWS = 16
NA_TQ = NA_QROWS * GRID_W
NA_TK = NA_KROWS * GRID_W

VMEM_LIMIT_BYTES = 56 * 1024 * 1024
FFN_TM = 1024
FFN_TF = 512
FFN_VMEM_LIMIT_BYTES = 60 * 1024 * 1024
MERGE_TN = 256


def _params(sem):
    return pltpu.CompilerParams(dimension_semantics=sem, vmem_limit_bytes=VMEM_LIMIT_BYTES)


def _rms(xf, g):
    ms = jnp.mean(xf * xf, axis=-1, keepdims=True)
    return xf * lax.rsqrt(ms + EPS) * g


def _dot(a, b):
    return jnp.dot(a, b, preferred_element_type=F32)


def _pick_tile(n, want):
    t = min(n, want)
    while n % t:
        t //= 2
    return t


def _ffn_kernel(x_ref, g_ref, wg_ref, wu_ref, wd_ref, o_ref, h_ref):
    j = pl.program_id(1)

    def down():
        h = h_ref[...]
        a = _dot(h, wg_ref[...])
        u = _dot(h, wu_ref[...])
        act = (a * jax.nn.sigmoid(a) * u).astype(BF16)
        return _dot(act, wd_ref[...])

    @pl.when(j == 0)
    def _():
        h_ref[...] = _rms(x_ref[...], g_ref[...]).astype(BF16)
        o_ref[...] = down()

    @pl.when(j > 0)
    def _():
        o_ref[...] += down()

    @pl.when(j == pl.num_programs(1) - 1)
    def _():
        o_ref[...] = x_ref[...] + 0.5 * o_ref[...]


def _ffn(x, g, wg, wu, wd):
    t, d = x.shape
    f = wg.shape[1]
    tm = _pick_tile(t, FFN_TM)
    tf = _pick_tile(f, FFN_TF)
    return pl.pallas_call(
        _ffn_kernel,
        out_shape=jax.ShapeDtypeStruct((t, d), F32),
        grid=(t // tm, f // tf),
        in_specs=[
            pl.BlockSpec((tm, d), lambda i, j: (i, 0)),
            pl.BlockSpec((1, d), lambda i, j: (0, 0)),
            pl.BlockSpec((d, tf), lambda i, j: (0, j)),
            pl.BlockSpec((d, tf), lambda i, j: (0, j)),
            pl.BlockSpec((tf, d), lambda i, j: (j, 0)),
        ],
        out_specs=pl.BlockSpec((tm, d), lambda i, j: (i, 0)),
        scratch_shapes=[pltpu.VMEM((tm, d), BF16)],
        compiler_params=pltpu.CompilerParams(dimension_semantics=("parallel", "arbitrary"),
                                             vmem_limit_bytes=FFN_VMEM_LIMIT_BYTES),
        name="ffn",
    )(x, g, wg, wu, wd)


def _swap_rotary_halves(y):
    lane = lax.broadcasted_iota(jnp.int32, y.shape, 1)
    first = (lane & 32) == 0
    return jnp.where(first, pltpu.roll(y, 96, 1), pltpu.roll(y, 32, 1))


def _qkv_kernel(x_ref, g_ref, w_ref, gain_ref, cos_ref, sin_ref, qv_ref, kt_ref):
    h = _rms(x_ref[...], g_ref[...]).astype(BF16)
    for h0, nh, norm, rope, is_key, slot0 in QKV_SEGMENTS:
        c0 = h0 * HEAD_DIM
        y = _dot(h, w_ref[:, c0:c0 + nh * HEAD_DIM])
        for hh in range(nh):
            lo = hh * HEAD_DIM
            yh = y[:, lo:lo + HEAD_DIM]
            if norm:
                yh = _rms(yh, gain_ref[:, c0 + lo:c0 + lo + HEAD_DIM])
            if rope:
                yh = yh * cos_ref[...] + _swap_rotary_halves(yh) * sin_ref[...]
            if is_key:
                kt_ref[slot0 + hh] = yh.T.astype(BF16)
            else:
                qv_ref[slot0 + hh] = yh.astype(BF16)


def _qkv(x, g, w_in, gain, cos, sin, seq):
    t, d = x.shape
    tm = _pick_tile(seq, 256)
    nseq = seq // tm
    return pl.pallas_call(
        _qkv_kernel,
        out_shape=(jax.ShapeDtypeStruct((N_QV, t, HEAD_DIM), BF16),
                   jax.ShapeDtypeStruct((N_KT, HEAD_DIM, t), BF16)),
        grid=(t // tm,),
        in_specs=[
            pl.BlockSpec((tm, d), lambda i: (i, 0)),
            pl.BlockSpec((1, d), lambda i: (0, 0)),
            pl.BlockSpec((d, D_IN), lambda i: (0, 0), pipeline_mode=pl.Buffered(1)),
            pl.BlockSpec((1, D_IN), lambda i: (0, 0)),
            pl.BlockSpec((tm, HEAD_DIM), lambda i: (i % nseq, 0)),
            pl.BlockSpec((tm, HEAD_DIM), lambda i: (i % nseq, 0)),
        ],
        out_specs=(pl.BlockSpec((N_QV, tm, HEAD_DIM), lambda i: (0, i, 0)),
                   pl.BlockSpec((N_KT, HEAD_DIM, tm), lambda i: (0, 0, i))),
        compiler_params=_params(("parallel",)),
        name="qkv",
    )(x, g, w_in, gain, cos, sin)


def _edge_variant(i, nblk):
    return jnp.where(i == 0, 0, jnp.where(i == nblk - 1, 2, 1))


def _two_stage_loop(nblk, logits, finish, sa_ref, sb_ref):
    logits(0, sa_ref)

    def body(j, carry):
        i = 2 * j
        logits(i + 1, sb_ref)
        finish(i, sa_ref)
        logits(i + 2, sa_ref)
        finish(i + 1, sb_ref)
        return carry

    lax.fori_loop(0, nblk // 2 - 1, body, 0)
    logits(nblk - 1, sb_ref)
    finish(nblk - 2, sa_ref)
    finish(nblk - 1, sb_ref)


def _na_kernel(q_ref, kt_ref, v_ref, bias_ref, o_ref, v1_ref, sa_ref, sb_ref, *, rows):
    nblk = q_ref.shape[0] // NA_TQ
    v1_ref[:, :HEAD_DIM] = v_ref[...]
    v1_ref[:, HEAD_DIM:] = jnp.ones(v_ref.shape, BF16)

    def window(i):
        start_row = jnp.clip(i * NA_QROWS - NA_ROWS // 2, 0, rows - NA_KROWS)
        return pl.multiple_of(start_row * GRID_W, (NA_ROWS // 2) * GRID_W)

    def logits(i, dst_ref):
        q = q_ref[pl.ds(pl.multiple_of(i * NA_TQ, NA_TQ), NA_TQ), :]
        dst_ref[...] = _dot(q, kt_ref[:, pl.ds(window(i), NA_TK)])

    def finish(i, src_ref):
        s = src_ref[...] + bias_ref[_edge_variant(i, nblk)]
        m = jnp.max(s, axis=-1, keepdims=True)
        p = jnp.exp2(s - m).astype(BF16)
        pv = _dot(p, v1_ref[pl.ds(window(i), NA_TK), :])
        o = pv[:, :HEAD_DIM] / pv[:, HEAD_DIM:]
        o_ref[pl.ds(pl.multiple_of(i * NA_TQ, NA_TQ), NA_TQ), :] = o.astype(BF16)

    _two_stage_loop(nblk, logits, finish, sa_ref, sb_ref)


def _na(qv, kt, bias, batch, seq):
    rows = seq // GRID_W
    assert seq % (2 * NA_TQ) == 0 and rows >= NA_KROWS
    return pl.pallas_call(
        functools.partial(_na_kernel, rows=rows),
        out_shape=jax.ShapeDtypeStruct((NA_HEADS, batch, seq, HEAD_DIM), BF16),
        grid=(NA_HEADS, batch),
        in_specs=[
            pl.BlockSpec((None, None, seq, HEAD_DIM), lambda h, b: (NA_Q0 + h, b, 0, 0)),
            pl.BlockSpec((None, HEAD_DIM, seq), lambda h, b: (NA_K0 + h, 0, b)),
            pl.BlockSpec((None, None, seq, HEAD_DIM), lambda h, b: (NA_V0 + h, b, 0, 0)),
            pl.BlockSpec((None, 3, NA_TQ, NA_TK), lambda h, b: (h, 0, 0, 0)),
        ],
        out_specs=pl.BlockSpec((None, None, seq, HEAD_DIM), lambda h, b: (h, b, 0, 0)),
        scratch_shapes=[
            pltpu.VMEM((seq, 2 * HEAD_DIM), BF16),
            pltpu.VMEM((NA_TQ, NA_TK), F32),
            pltpu.VMEM((NA_TQ, NA_TK), F32),
        ],
        compiler_params=_params(("parallel", "parallel")),
        name="na",
    )(qv, kt, qv, bias)


def _na_bias_table(rel_bias):
    rel = rel_bias.astype(F32) * LOG2E
    rel_pad = jnp.pad(rel, ((0, 0), (0, 0), (GRID_W, GRID_W)))
    off = GRID_W + NA_COLS - 1
    t1 = jnp.stack([rel_pad[:, :, off - c:off - c + GRID_W] for c in range(GRID_W)], axis=1)
    cq = np.arange(GRID_W)[:, None]
    kc = np.arange(GRID_W)[None, :]
    cs = np.clip(cq - NA_COLS // 2, 0, GRID_W - NA_COLS)
    col_ok = (kc >= cs) & (kc < cs + NA_COLS)
    t1 = jnp.where(col_ok[None, :, None, :], t1, MASKED)

    rows = 3 * NA_QROWS
    blocks = []
    for v in range(3):
        start_row = int(np.clip(v * NA_QROWS - NA_ROWS // 2, 0, rows - NA_KROWS))
        for qr in range(NA_QROWS):
            r = v * NA_QROWS + qr
            rs = int(np.clip(r - NA_ROWS // 2, 0, rows - NA_ROWS))
            lo = rs - start_row
            dr0 = rs - r + NA_ROWS - 1
            blk = t1[:, :, dr0:dr0 + NA_ROWS, :]
            blocks.append(jnp.pad(blk, ((0, 0), (0, 0), (lo, NA_KROWS - NA_ROWS - lo), (0, 0)),
                                  constant_values=MASKED))
    table = jnp.stack(blocks, axis=1)
    return table.reshape(NA_HEADS, 3, NA_TQ, NA_TK)


def _ga_kernel(bound_ref, q_ref, kt_ref, v_ref, o_ref, v1_ref, m_ref, acc_ref, sa_ref, sb_ref,
               *, tq, tk, group, bounded):
    @pl.when(pl.program_id(2) == 0)
    def _():
        v1_ref[:, :HEAD_DIM] = v_ref[...]
        v1_ref[:, HEAD_DIM:] = jnp.ones(v_ref.shape, BF16)

    m_ref[...] = jnp.full(m_ref.shape, MASKED, F32)
    acc_ref[...] = jnp.zeros(acc_ref.shape, F32)
    lane_tiles = tk // HEAD_DIM
    n_chunks = kt_ref.shape[1] // tk

    def logits(c, dst_ref):
        start = pl.multiple_of(c * tk, tk)
        q = q_ref[...].reshape(group * tq, HEAD_DIM)
        s = _dot(q, kt_ref[:, pl.ds(start, tk)])
        if bounded:
            dst_ref[...] = jnp.exp2(s - bound_ref[0]).astype(BF16)
        else:
            dst_ref[...] = s

    def accumulate(c, src_ref):
        start = pl.multiple_of(c * tk, tk)
        if bounded:
            acc_ref[...] += _dot(src_ref[...], v1_ref[pl.ds(start, tk), :])
            return
        s = src_ref[...]
        m_prev = m_ref[...]
        m_new = jnp.maximum(m_prev, jnp.max(s, axis=-1, keepdims=True))
        alpha = jnp.exp2(m_prev - m_new)
        p = jnp.exp2(s - jnp.concatenate([m_new] * lane_tiles, axis=1))
        pv = _dot(p.astype(BF16), v1_ref[pl.ds(start, tk), :])
        acc_ref[...] = jnp.concatenate([alpha, alpha], axis=1) * acc_ref[...] + pv
        m_ref[...] = m_new

    _two_stage_loop(n_chunks, logits, accumulate, sa_ref, sb_ref)
    o = acc_ref[:, :HEAD_DIM] / acc_ref[:, HEAD_DIM:]
    o_ref[...] = o.reshape(group, tq, HEAD_DIM).astype(BF16)


def _ga(bound, qv, kt, batch, seq, bounded):
    group = GA_HEADS // GA_KV_HEADS
    tq = _pick_tile(seq, 256)
    tk = _pick_tile(seq, 512)
    assert (seq // tk) % 2 == 0
    score_dtype = BF16 if bounded else F32
    return pl.pallas_call(
        functools.partial(_ga_kernel, tq=tq, tk=tk, group=group, bounded=bounded),
        out_shape=jax.ShapeDtypeStruct((GA_HEADS, batch, seq, HEAD_DIM), BF16),
        grid=(batch, GA_KV_HEADS, seq // tq),
        in_specs=[
            pl.BlockSpec(memory_space=pltpu.SMEM),
            pl.BlockSpec((group, None, tq, HEAD_DIM), lambda b, kv, i: (GA_Q0 // group + kv, b, i, 0)),
            pl.BlockSpec((None, HEAD_DIM, seq), lambda b, kv, i: (GA_K0 + kv, 0, b)),
            pl.BlockSpec((None, None, seq, HEAD_DIM), lambda b, kv, i: (GA_V0 + kv, b, 0, 0)),
        ],
        out_specs=pl.BlockSpec((group, None, tq, HEAD_DIM), lambda b, kv, i: (kv, b, i, 0)),
        scratch_shapes=[
            pltpu.VMEM((seq, 2 * HEAD_DIM), BF16),
            pltpu.VMEM((group * tq, HEAD_DIM), F32),
            pltpu.VMEM((group * tq, 2 * HEAD_DIM), F32),
            pltpu.VMEM((group * tq, tk), score_dtype),
            pltpu.VMEM((group * tq, tk), score_dtype),
        ],
        compiler_params=_params(("parallel", "parallel", "arbitrary")),
        name="ga_bounded" if bounded else "ga",
    )(bound, qv, kt, qv)


def _sw_kernel(q_ref, kt_ref, v_ref, bias_ref, sink_ref, o_ref, v1_ref, sa_ref, sb_ref, *, seq, group):
    span = Q_BLOCK + 2 * WINDOW
    nblk_seq = seq // Q_BLOCK
    nblk = q_ref.shape[1] // Q_BLOCK
    first = pl.program_id(2) * nblk

    @pl.when(pl.program_id(2) == 0)
    def _():
        v1_ref[:, :HEAD_DIM] = v_ref[...]
        v1_ref[:, HEAD_DIM:] = jnp.ones(v_ref.shape, BF16)

    def window(n):
        return pl.multiple_of(jnp.clip((first + n) * Q_BLOCK - WINDOW, 0, seq - span), Q_BLOCK)

    def rows(n):
        return pl.ds(pl.multiple_of(n * Q_BLOCK, Q_BLOCK), Q_BLOCK)

    def logits(n, dst_ref):
        q = q_ref[:, rows(n), :].reshape(group * Q_BLOCK, HEAD_DIM)
        dst_ref[...] = _dot(q, kt_ref[:, pl.ds(window(n), span)])

    def finish(n, src_ref):
        sink = sink_ref[...]
        s = src_ref[...] + bias_ref[_edge_variant(first + n, nblk_seq)]
        m = jnp.maximum(jnp.max(s, axis=-1, keepdims=True), sink)
        p = jnp.exp2(s - jnp.concatenate([m] * (span // HEAD_DIM), axis=1)).astype(BF16)
        pv = _dot(p, v1_ref[pl.ds(window(n), span), :])
        o = pv[:, :HEAD_DIM] / (pv[:, HEAD_DIM:] + jnp.exp2(sink - m))
        o_ref[:, rows(n), :] = o.reshape(group, Q_BLOCK, HEAD_DIM).astype(BF16)

    _two_stage_loop(nblk, logits, finish, sa_ref, sb_ref)


def _sw_bias_table():
    group = SW_HEADS // SW_KV_HEADS
    span = Q_BLOCK + 2 * WINDOW
    slopes = 2.0 ** (-8.0 * np.arange(1, SW_HEADS + 1, dtype=np.float64) / SW_HEADS)
    qi = np.arange(Q_BLOCK)[:, None]
    kj = np.arange(span)[None, :]
    table = np.empty((SW_KV_HEADS, 3, group, Q_BLOCK, span), np.float32)
    for v in range(3):
        dist = np.abs(qi - (kj - v * WINDOW))
        for h in range(SW_HEADS):
            table[h // group, v, h % group] = np.where(dist <= WINDOW, -LOG2E * slopes[h] * dist, MASKED)
    return jnp.asarray(table.reshape(SW_KV_HEADS, 3, group * Q_BLOCK, span))


def _sw(qv, kt, bias, sink, batch, seq):
    group = SW_HEADS // SW_KV_HEADS
    span = Q_BLOCK + 2 * WINDOW
    tq = _pick_tile(seq, 16 * Q_BLOCK)
    assert tq % (2 * Q_BLOCK) == 0 and seq >= span
    rows = group * Q_BLOCK
    return pl.pallas_call(
        functools.partial(_sw_kernel, seq=seq, group=group),
        out_shape=jax.ShapeDtypeStruct((SW_HEADS, batch, seq, HEAD_DIM), BF16),
        grid=(batch, SW_KV_HEADS, seq // tq),
        in_specs=[
            pl.BlockSpec((group, None, tq, HEAD_DIM), lambda b, kv, i: (SW_Q0 // group + kv, b, i, 0)),
            pl.BlockSpec((None, HEAD_DIM, seq), lambda b, kv, i: (SW_K0 + kv, 0, b)),
            pl.BlockSpec((None, None, seq, HEAD_DIM), lambda b, kv, i: (SW_V0 + kv, b, 0, 0)),
            pl.BlockSpec((None, 3, rows, span), lambda b, kv, i: (kv, 0, 0, 0)),
            pl.BlockSpec((None, rows, HEAD_DIM), lambda b, kv, i: (kv, 0, 0)),
        ],
        out_specs=pl.BlockSpec((group, None, tq, HEAD_DIM), lambda b, kv, i: (kv, b, i, 0)),
        scratch_shapes=[
            pltpu.VMEM((seq, 2 * HEAD_DIM), BF16),
            pltpu.VMEM((rows, span), F32),
            pltpu.VMEM((rows, span), F32),
        ],
        compiler_params=_params(("parallel", "parallel", "arbitrary")),
        name="sw",
    )(qv, kt, qv, bias, sink)


def _merge_kernel(x_ref, g_ref, oa_ref, ob_ref, oc_ref, wg_ref, wb_ref, wo_ref, o_ref, h_ref, merged_ref):
    j = pl.program_id(1)

    @pl.when(j == 0)
    def _():
        h_ref[...] = _rms(x_ref[...], g_ref[...]).astype(BF16)

    h = h_ref[...]
    merged = None
    for n, br_ref in enumerate((oa_ref, ob_ref, oc_ref)):
        branch = jnp.concatenate([br_ref[hd] for hd in range(br_ref.shape[0])], axis=1)
        term = jax.nn.sigmoid(_dot(h, wg_ref[n])) * _dot(branch, wb_ref[n])
        merged = term if merged is None else merged + term
    merged_ref[j] = merged.astype(BF16)

    @pl.when(j == pl.num_programs(1) - 1)
    def _():
        merged_all = jnp.concatenate([merged_ref[c] for c in range(merged_ref.shape[0])], axis=1)
        o_ref[...] = x_ref[...] + _dot(merged_all, wo_ref[...])


def _merge(x, g, oa, ob, oc, w_gate, w_branch, w_out):
    t, d = x.shape
    nh = oa.shape[0]
    bw = nh * HEAD_DIM
    tm = _pick_tile(t, 512)
    tn = _pick_tile(d, MERGE_TN)
    return pl.pallas_call(
        _merge_kernel,
        out_shape=jax.ShapeDtypeStruct((t, d), F32),
        grid=(t // tm, d // tn),
        in_specs=[
            pl.BlockSpec((tm, d), lambda i, j: (i, 0)),
            pl.BlockSpec((1, d), lambda i, j: (0, 0)),
            pl.BlockSpec((nh, tm, HEAD_DIM), lambda i, j: (0, i, 0)),
            pl.BlockSpec((nh, tm, HEAD_DIM), lambda i, j: (0, i, 0)),
            pl.BlockSpec((nh, tm, HEAD_DIM), lambda i, j: (0, i, 0)),
            pl.BlockSpec((3, d, tn), lambda i, j: (0, 0, j)),
            pl.BlockSpec((3, bw, tn), lambda i, j: (0, 0, j)),
            pl.BlockSpec((d, d), lambda i, j: (0, 0), pipeline_mode=pl.Buffered(1)),
        ],
        out_specs=pl.BlockSpec((tm, d), lambda i, j: (i, 0)),
        scratch_shapes=[pltpu.VMEM((tm, d), BF16), pltpu.VMEM((d // tn, tm, tn), BF16)],
        compiler_params=_params(("parallel", "arbitrary")),
        name="merge",
    )(x, g, oa, ob, oc, w_gate, w_branch, w_out)


def _rope_tables(seq):
    quarter = HEAD_DIM // 4
    freqs = ROPE_THETA ** (-jnp.arange(quarter, dtype=F32) / quarter)
    t = jnp.arange(seq)
    row = (t // GRID_W).astype(F32)
    col = (t % GRID_W).astype(F32)
    ang_r = row[:, None] * freqs[None, :]
    ang_c = col[:, None] * freqs[None, :]
    cos = jnp.concatenate([jnp.cos(ang_r)] * 2 + [jnp.cos(ang_c)] * 2, axis=-1)
    sin = jnp.concatenate([-jnp.sin(ang_r), jnp.sin(ang_r), -jnp.sin(ang_c), jnp.sin(ang_c)], axis=-1)
    return cos, sin


def _gain_row(na_q, na_k, ga_q, ga_k, sw_q, sw_k):
    scale = HEAD_DIM ** -0.5 * LOG2E
    one = jnp.ones((HEAD_DIM,), F32)
    parts = ([na_q * scale] * 8 + [na_k] * 8 + [one] * 8 + [ga_q * scale] * 8 + [ga_k] * 2 + [one] * 2
             + [sw_q * scale] * 8 + [sw_k] * 2 + [one] * 2)
    return jnp.concatenate([p.astype(F32) for p in parts]).reshape(1, D_IN)


def _ga_shift(ga_q, ga_k):
    fold = HEAD_DIM ** -0.5 * LOG2E
    bound = (HEAD_DIM * fold * 1.02 * jnp.max(jnp.abs(ga_q.astype(F32)))
             * jnp.max(jnp.abs(ga_k.astype(F32))))
    return jnp.zeros((1,), F32), 2.0 * bound <= MAX_EXP2_RANGE


def _sw_sink_rows(sink, group):
    per_head = (sink.astype(F32) * LOG2E).reshape(SW_KV_HEADS, group, 1, 1)
    return jnp.broadcast_to(per_head, (SW_KV_HEADS, group, Q_BLOCK, HEAD_DIM)).reshape(
        SW_KV_HEADS, group * Q_BLOCK, HEAD_DIM)


def _trunk(x3, layers, rope):
    batch, seq, d = x3.shape
    x = x3.reshape(batch * seq, d)
    cos, sin = rope
    for p in layers:
        x = _ffn(x, p["ffn1_norm"], p["ffn1_wg"], p["ffn1_wu"], p["ffn1_wd"])
        qv, kt = _qkv(x, p["mix_norm"], p["w_in"], p["gain"], cos, sin, seq)
        qv = qv.reshape(N_QV, batch, seq, HEAD_DIM)
        shift, shift_ok = p["ga_shift"]
        oa = _na(qv, kt, p["na_bias"], batch, seq)
        ob = lax.cond(shift_ok,
                      lambda: _ga(shift, qv, kt, batch, seq, True),
                      lambda: _ga(shift, qv, kt, batch, seq, False))
        oc = _sw(qv, kt, p["sw_bias"], p["sw_sink"], batch, seq)
        oa = oa.reshape(NA_HEADS, batch * seq, HEAD_DIM)
        ob = ob.reshape(GA_HEADS, batch * seq, HEAD_DIM)
        oc = oc.reshape(SW_HEADS, batch * seq, HEAD_DIM)
        x = _merge(x, p["mix_norm"], oa, ob, oc, p["w_gate"], p["w_branch"], p["w_out"])
        x = _ffn(x, p["ffn2_norm"], p["ffn2_wg"], p["ffn2_wu"], p["ffn2_wd"])
    return x.reshape(batch, seq, d)


def kernel(x_prompt, x_sample, ffn1_norm, ffn1_wg, ffn1_wu, ffn1_wd, mix_norm, w_in, w_gate, na_q_norm, na_k_norm, na_rel_bias, ga_q_norm, ga_k_norm, sw_q_norm, sw_k_norm, sw_sink, w_branch, w_out, ffn2_norm, ffn2_wg, ffn2_wu, ffn2_wd):
    depth = w_in.shape[0]
    d = x_prompt.shape[-1]
    sw_group = SW_HEADS // SW_KV_HEADS
    sw_bias = _sw_bias_table()
    layers = []
    for l in range(depth):
        layers.append(dict(
            ffn1_norm=ffn1_norm[l].reshape(1, d).astype(F32),
            ffn1_wg=ffn1_wg[l].astype(BF16), ffn1_wu=ffn1_wu[l].astype(BF16), ffn1_wd=ffn1_wd[l].astype(BF16),
            mix_norm=mix_norm[l].reshape(1, d).astype(F32),
            w_in=w_in[l].astype(BF16),
            gain=_gain_row(na_q_norm[l], na_k_norm[l], ga_q_norm[l], ga_k_norm[l], sw_q_norm[l], sw_k_norm[l]),
            na_bias=_na_bias_table(na_rel_bias[l]),
            ga_shift=_ga_shift(ga_q_norm[l], ga_k_norm[l]),
            sw_bias=sw_bias,
            sw_sink=_sw_sink_rows(sw_sink[l], sw_group),
            w_gate=w_gate[l].astype(BF16), w_branch=w_branch[l].astype(BF16), w_out=w_out[l].astype(BF16),
            ffn2_norm=ffn2_norm[l].reshape(1, d).astype(F32),
            ffn2_wg=ffn2_wg[l].astype(BF16), ffn2_wu=ffn2_wu[l].astype(BF16), ffn2_wd=ffn2_wd[l].astype(BF16),
        ))
    outs = []
    for x3 in (x_prompt, x_sample):
        outs.append(_trunk(x3, layers, _rope_tables(x3.shape[1])))
    return tuple(outs)
```

```python
import functools
import math

import numpy as np
import jax
import jax.numpy as jnp
from jax import lax
from jax.experimental import pallas as pl
from jax.experimental.pallas import tpu as pltpu

F32 = jnp.float32
BF16 = jnp.bfloat16

EPS = 1e-6
HEAD_DIM = 128
GRID_W = 64
NA_HEADS = 8
NA_ROWS = 8
NA_COLS = 16
GA_HEADS = 8
GA_KV_HEADS = 2
SW_HEADS = 8
SW_KV_HEADS = 2
WINDOW = 128
Q_BLOCK = 128
ROPE_THETA = 10000.0
MASKED = -1e30
LOG2E = math.log2(math.e)
MAX_EXP2_RANGE = 100.0

N_IN_HEADS = 48
D_IN = N_IN_HEADS * HEAD_DIM

NA_Q0, GA_Q0, SW_Q0, NA_V0, GA_V0, SW_V0 = 0, 8, 16, 24, 32, 34
N_QV = 36
NA_K0, GA_K0, SW_K0 = 0, 8, 10
N_KT = 12

QKV_SEGMENTS = (
    (0, 8, True, False, False, NA_Q0), (8, 8, True, False, True, NA_K0), (16, 8, False, False, False, NA_V0),
    (24, 8, True, True, False, GA_Q0), (32, 2, True, True, True, GA_K0), (34, 2, False, False, False, GA_V0),
    (36, 8, True, False, False, SW_Q0), (44, 2, True, False, True, SW_K0), (46, 2, False, False, False, SW_V0),
)

NA_QROWS = 8
NA_KROWS = 16
NA_TQ = NA_QROWS * GRID_W
NA_TK = NA_KROWS * GRID_W

VMEM_LIMIT_BYTES = 56 * 1024 * 1024
FFN_TM = 1024
FFN_TF = 512
WIDE_VMEM_LIMIT_BYTES = 60 * 1024 * 1024
MERGE_TN = 256


def _params(sem):
    return pltpu.CompilerParams(dimension_semantics=sem, vmem_limit_bytes=VMEM_LIMIT_BYTES)


def _rms(xf, g):
    ms = jnp.mean(xf * xf, axis=-1, keepdims=True)
    return xf * lax.rsqrt(ms + EPS) * g


def _dot(a, b):
    return jnp.dot(a, b, preferred_element_type=F32)


def _pick_tile(n, want):
    t = min(n, want)
    while n % t:
        t //= 2
    return t


def _ffn_kernel(x_ref, g_ref, wg_ref, wu_ref, wd_ref, o_ref, h_ref):
    j = pl.program_id(1)

    def down():
        h = h_ref[...]
        a = _dot(h, wg_ref[...])
        u = _dot(h, wu_ref[...])
        act = (a * jax.nn.sigmoid(a) * u).astype(BF16)
        return _dot(act, wd_ref[...])

    @pl.when(j == 0)
    def _():
        h_ref[...] = _rms(x_ref[...], g_ref[...]).astype(BF16)
        o_ref[...] = down()

    @pl.when(j > 0)
    def _():
        o_ref[...] += down()

    @pl.when(j == pl.num_programs(1) - 1)
    def _():
        o_ref[...] = x_ref[...] + 0.5 * o_ref[...]


def _ffn(x, g, wg, wu, wd):
    t, d = x.shape
    f = wg.shape[1]
    tm = _pick_tile(t, FFN_TM)
    tf = _pick_tile(f, FFN_TF)
    return pl.pallas_call(
        _ffn_kernel,
        out_shape=jax.ShapeDtypeStruct((t, d), F32),
        grid=(t // tm, f // tf),
        in_specs=[
            pl.BlockSpec((tm, d), lambda i, j: (i, 0)),
            pl.BlockSpec((1, d), lambda i, j: (0, 0)),
            pl.BlockSpec((d, tf), lambda i, j: (0, j)),
            pl.BlockSpec((d, tf), lambda i, j: (0, j)),
            pl.BlockSpec((tf, d), lambda i, j: (j, 0)),
        ],
        out_specs=pl.BlockSpec((tm, d), lambda i, j: (i, 0)),
        scratch_shapes=[pltpu.VMEM((tm, d), BF16)],
        compiler_params=pltpu.CompilerParams(dimension_semantics=("parallel", "arbitrary"),
                                             vmem_limit_bytes=WIDE_VMEM_LIMIT_BYTES),
        name="ffn",
    )(x, g, wg, wu, wd)


def _swap_rotary_halves(y):
    lane = lax.broadcasted_iota(jnp.int32, y.shape, 1)
    first = (lane & 32) == 0
    return jnp.where(first, pltpu.roll(y, 96, 1), pltpu.roll(y, 32, 1))


def _qkv_kernel(x_ref, g_ref, w_ref, gain_ref, cos_ref, sin_ref, qv_ref, kt_ref):
    h = _rms(x_ref[...], g_ref[...]).astype(BF16)
    for h0, nh, norm, rope, is_key, slot0 in QKV_SEGMENTS:
        c0 = h0 * HEAD_DIM
        y = _dot(h, w_ref[:, c0:c0 + nh * HEAD_DIM])
        for hh in range(nh):
            lo = hh * HEAD_DIM
            yh = y[:, lo:lo + HEAD_DIM]
            if norm:
                yh = _rms(yh, gain_ref[:, c0 + lo:c0 + lo + HEAD_DIM])
            if rope:
                yh = yh * cos_ref[...] + _swap_rotary_halves(yh) * sin_ref[...]
            if is_key:
                kt_ref[slot0 + hh] = yh.T.astype(BF16)
            else:
                qv_ref[slot0 + hh] = yh.astype(BF16)


def _qkv(x, g, w_in, gain, cos, sin, seq):
    t, d = x.shape
    tm = _pick_tile(seq, 256)
    nseq = seq // tm
    return pl.pallas_call(
        _qkv_kernel,
        out_shape=(jax.ShapeDtypeStruct((N_QV, t, HEAD_DIM), BF16),
                   jax.ShapeDtypeStruct((N_KT, HEAD_DIM, t), BF16)),
        grid=(t // tm,),
        in_specs=[
            pl.BlockSpec((tm, d), lambda i: (i, 0)),
            pl.BlockSpec((1, d), lambda i: (0, 0)),
            pl.BlockSpec((d, D_IN), lambda i: (0, 0), pipeline_mode=pl.Buffered(1)),
            pl.BlockSpec((1, D_IN), lambda i: (0, 0)),
            pl.BlockSpec((tm, HEAD_DIM), lambda i: (i % nseq, 0)),
            pl.BlockSpec((tm, HEAD_DIM), lambda i: (i % nseq, 0)),
        ],
        out_specs=(pl.BlockSpec((N_QV, tm, HEAD_DIM), lambda i: (0, i, 0)),
                   pl.BlockSpec((N_KT, HEAD_DIM, tm), lambda i: (0, 0, i))),
        compiler_params=_params(("parallel",)),
        name="qkv",
    )(x, g, w_in, gain, cos, sin)


def _edge_variant(i, nblk):
    return jnp.where(i == 0, 0, jnp.where(i == nblk - 1, 2, 1))


def _two_stage_loop(nblk, logits, finish, sa_ref, sb_ref):
    logits(0, sa_ref)

    def body(j, carry):
        i = 2 * j
        logits(i + 1, sb_ref)
        finish(i, sa_ref)
        logits(i + 2, sa_ref)
        finish(i + 1, sb_ref)
        return carry

    lax.fori_loop(0, nblk // 2 - 1, body, 0)
    logits(nblk - 1, sb_ref)
    finish(nblk - 2, sa_ref)
    finish(nblk - 1, sb_ref)


def _na_kernel(q_ref, kt_ref, v_ref, bias_ref, o_ref, v1_ref, sa_ref, sb_ref, *, rows):
    nblk = q_ref.shape[0] // NA_TQ
    v1_ref[:, :HEAD_DIM] = v_ref[...]
    v1_ref[:, HEAD_DIM:] = jnp.ones(v_ref.shape, BF16)

    def window(i):
        start_row = jnp.clip(i * NA_QROWS - NA_ROWS // 2, 0, rows - NA_KROWS)
        return pl.multiple_of(start_row * GRID_W, (NA_ROWS // 2) * GRID_W)

    def logits(i, dst_ref):
        q = q_ref[pl.ds(pl.multiple_of(i * NA_TQ, NA_TQ), NA_TQ), :]
        dst_ref[...] = _dot(q, kt_ref[:, pl.ds(window(i), NA_TK)])

    def finish(i, src_ref):
        s = src_ref[...] + bias_ref[_edge_variant(i, nblk)]
        m = jnp.max(s, axis=-1, keepdims=True)
        p = jnp.exp2(s - m).astype(BF16)
        pv = _dot(p, v1_ref[pl.ds(window(i), NA_TK), :])
        o = pv[:, :HEAD_DIM] / pv[:, HEAD_DIM:]
        o_ref[pl.ds(pl.multiple_of(i * NA_TQ, NA_TQ), NA_TQ), :] = o.astype(BF16)

    _two_stage_loop(nblk, logits, finish, sa_ref, sb_ref)


def _na(qv, kt, bias, batch, seq):
    rows = seq // GRID_W
    assert seq % (2 * NA_TQ) == 0 and rows >= NA_KROWS
    return pl.pallas_call(
        functools.partial(_na_kernel, rows=rows),
        out_shape=jax.ShapeDtypeStruct((NA_HEADS, batch, seq, HEAD_DIM), BF16),
        grid=(NA_HEADS, batch),
        in_specs=[
            pl.BlockSpec((None, None, seq, HEAD_DIM), lambda h, b: (NA_Q0 + h, b, 0, 0)),
            pl.BlockSpec((None, HEAD_DIM, seq), lambda h, b: (NA_K0 + h, 0, b)),
            pl.BlockSpec((None, None, seq, HEAD_DIM), lambda h, b: (NA_V0 + h, b, 0, 0)),
            pl.BlockSpec((None, 3, NA_TQ, NA_TK), lambda h, b: (h, 0, 0, 0)),
        ],
        out_specs=pl.BlockSpec((None, None, seq, HEAD_DIM), lambda h, b: (h, b, 0, 0)),
        scratch_shapes=[
            pltpu.VMEM((seq, 2 * HEAD_DIM), BF16),
            pltpu.VMEM((NA_TQ, NA_TK), F32),
            pltpu.VMEM((NA_TQ, NA_TK), F32),
        ],
        compiler_params=_params(("parallel", "parallel")),
        name="na",
    )(qv, kt, qv, bias)


def _na_bias_table(rel_bias):
    rel = rel_bias.astype(F32) * LOG2E
    rel_pad = jnp.pad(rel, ((0, 0), (0, 0), (GRID_W, GRID_W)))
    off = GRID_W + NA_COLS - 1
    t1 = jnp.stack([rel_pad[:, :, off - c:off - c + GRID_W] for c in range(GRID_W)], axis=1)
    cq = np.arange(GRID_W)[:, None]
    kc = np.arange(GRID_W)[None, :]
    cs = np.clip(cq - NA_COLS // 2, 0, GRID_W - NA_COLS)
    col_ok = (kc >= cs) & (kc < cs + NA_COLS)
    t1 = jnp.where(col_ok[None, :, None, :], t1, MASKED)

    rows = 3 * NA_QROWS
    blocks = []
    for v in range(3):
        start_row = int(np.clip(v * NA_QROWS - NA_ROWS // 2, 0, rows - NA_KROWS))
        for qr in range(NA_QROWS):
            r = v * NA_QROWS + qr
            rs = int(np.clip(r - NA_ROWS // 2, 0, rows - NA_ROWS))
            lo = rs - start_row
            dr0 = rs - r + NA_ROWS - 1
            blk = t1[:, :, dr0:dr0 + NA_ROWS, :]
            blocks.append(jnp.pad(blk, ((0, 0), (0, 0), (lo, NA_KROWS - NA_ROWS - lo), (0, 0)),
                                  constant_values=MASKED))
    table = jnp.stack(blocks, axis=1)
    return table.reshape(NA_HEADS, 3, NA_TQ, NA_TK)


def _ga_kernel(bound_ref, q_ref, kt_ref, v_ref, o_ref, v1_ref, m_ref, acc_ref, sa_ref, sb_ref,
               *, tq, tk, group, bounded):
    @pl.when(pl.program_id(2) == 0)
    def _():
        v1_ref[:, :HEAD_DIM] = v_ref[...]
        v1_ref[:, HEAD_DIM:] = jnp.ones(v_ref.shape, BF16)

    m_ref[...] = jnp.full(m_ref.shape, MASKED, F32)
    acc_ref[...] = jnp.zeros(acc_ref.shape, F32)
    lane_tiles = tk // HEAD_DIM
    n_chunks = kt_ref.shape[1] // tk

    def logits(c, dst_ref):
        start = pl.multiple_of(c * tk, tk)
        q = q_ref[...].reshape(group * tq, HEAD_DIM)
        s = _dot(q, kt_ref[:, pl.ds(start, tk)])
        if bounded:
            dst_ref[...] = jnp.exp2(s - bound_ref[0]).astype(BF16)
        else:
            dst_ref[...] = s

    def accumulate(c, src_ref):
        start = pl.multiple_of(c * tk, tk)
        if bounded:
            acc_ref[...] += _dot(src_ref[...], v1_ref[pl.ds(start, tk), :])
            return
        s = src_ref[...]
        m_prev = m_ref[...]
        m_new = jnp.maximum(m_prev, jnp.max(s, axis=-1, keepdims=True))
        alpha = jnp.exp2(m_prev - m_new)
        p = jnp.exp2(s - jnp.concatenate([m_new] * lane_tiles, axis=1))
        pv = _dot(p.astype(BF16), v1_ref[pl.ds(start, tk), :])
        acc_ref[...] = jnp.concatenate([alpha, alpha], axis=1) * acc_ref[...] + pv
        m_ref[...] = m_new

    _two_stage_loop(n_chunks, logits, accumulate, sa_ref, sb_ref)
    o = acc_ref[:, :HEAD_DIM] / acc_ref[:, HEAD_DIM:]
    o_ref[...] = o.reshape(group, tq, HEAD_DIM).astype(BF16)


def _ga(bound, qv, kt, batch, seq, bounded):
    group = GA_HEADS // GA_KV_HEADS
    tq = _pick_tile(seq, 256)
    tk = _pick_tile(seq, 512)
    assert (seq // tk) % 2 == 0
    score_dtype = BF16 if bounded else F32
    return pl.pallas_call(
        functools.partial(_ga_kernel, tq=tq, tk=tk, group=group, bounded=bounded),
        out_shape=jax.ShapeDtypeStruct((GA_HEADS, batch, seq, HEAD_DIM), BF16),
        grid=(batch, GA_KV_HEADS, seq // tq),
        in_specs=[
            pl.BlockSpec(memory_space=pltpu.SMEM),
            pl.BlockSpec((group, None, tq, HEAD_DIM), lambda b, kv, i: (GA_Q0 // group + kv, b, i, 0)),
            pl.BlockSpec((None, HEAD_DIM, seq), lambda b, kv, i: (GA_K0 + kv, 0, b)),
            pl.BlockSpec((None, None, seq, HEAD_DIM), lambda b, kv, i: (GA_V0 + kv, b, 0, 0)),
        ],
        out_specs=pl.BlockSpec((group, None, tq, HEAD_DIM), lambda b, kv, i: (kv, b, i, 0)),
        scratch_shapes=[
            pltpu.VMEM((seq, 2 * HEAD_DIM), BF16),
            pltpu.VMEM((group * tq, HEAD_DIM), F32),
            pltpu.VMEM((group * tq, 2 * HEAD_DIM), F32),
            pltpu.VMEM((group * tq, tk), score_dtype),
            pltpu.VMEM((group * tq, tk), score_dtype),
        ],
        compiler_params=_params(("parallel", "parallel", "arbitrary")),
        name="ga_bounded" if bounded else "ga",
    )(bound, qv, kt, qv)


def _sw_kernel(q_ref, kt_ref, v_ref, bias_ref, sink_ref, o_ref, v1_ref, sa_ref, sb_ref, *, seq, group):
    span = Q_BLOCK + 2 * WINDOW
    nblk_seq = seq // Q_BLOCK
    nblk = q_ref.shape[1] // Q_BLOCK
    first = pl.program_id(2) * nblk

    @pl.when(pl.program_id(2) == 0)
    def _():
        v1_ref[:, :HEAD_DIM] = v_ref[...]
        v1_ref[:, HEAD_DIM:] = jnp.ones(v_ref.shape, BF16)

    def window(n):
        return pl.multiple_of(jnp.clip((first + n) * Q_BLOCK - WINDOW, 0, seq - span), Q_BLOCK)

    def rows(n):
        return pl.ds(pl.multiple_of(n * Q_BLOCK, Q_BLOCK), Q_BLOCK)

    def logits(n, dst_ref):
        q = q_ref[:, rows(n), :].reshape(group * Q_BLOCK, HEAD_DIM)
        dst_ref[...] = _dot(q, kt_ref[:, pl.ds(window(n), span)])

    def finish(n, src_ref):
        sink = sink_ref[...]
        s = src_ref[...] + bias_ref[_edge_variant(first + n, nblk_seq)]
        m = jnp.maximum(jnp.max(s, axis=-1, keepdims=True), sink)
        p = jnp.exp2(s - jnp.concatenate([m] * (span // HEAD_DIM), axis=1)).astype(BF16)
        pv = _dot(p, v1_ref[pl.ds(window(n), span), :])
        o = pv[:, :HEAD_DIM] / (pv[:, HEAD_DIM:] + jnp.exp2(sink - m))
        o_ref[:, rows(n), :] = o.reshape(group, Q_BLOCK, HEAD_DIM).astype(BF16)

    _two_stage_loop(nblk, logits, finish, sa_ref, sb_ref)


def _sw_bias_table():
    group = SW_HEADS // SW_KV_HEADS
    span = Q_BLOCK + 2 * WINDOW
    slopes = 2.0 ** (-8.0 * np.arange(1, SW_HEADS + 1, dtype=np.float64) / SW_HEADS)
    qi = np.arange(Q_BLOCK)[:, None]
    kj = np.arange(span)[None, :]
    table = np.empty((SW_KV_HEADS, 3, group, Q_BLOCK, span), np.float32)
    for v in range(3):
        dist = np.abs(qi - (kj - v * WINDOW))
        for h in range(SW_HEADS):
            table[h // group, v, h % group] = np.where(dist <= WINDOW, -LOG2E * slopes[h] * dist, MASKED)
    return jnp.asarray(table.reshape(SW_KV_HEADS, 3, group * Q_BLOCK, span))


def _sw(qv, kt, bias, sink, batch, seq):
    group = SW_HEADS // SW_KV_HEADS
    span = Q_BLOCK + 2 * WINDOW
    tq = _pick_tile(seq, 16 * Q_BLOCK)
    assert tq % (2 * Q_BLOCK) == 0 and seq >= span
    rows = group * Q_BLOCK
    return pl.pallas_call(
        functools.partial(_sw_kernel, seq=seq, group=group),
        out_shape=jax.ShapeDtypeStruct((SW_HEADS, batch, seq, HEAD_DIM), BF16),
        grid=(batch, SW_KV_HEADS, seq // tq),
        in_specs=[
            pl.BlockSpec((group, None, tq, HEAD_DIM), lambda b, kv, i: (SW_Q0 // group + kv, b, i, 0)),
            pl.BlockSpec((None, HEAD_DIM, seq), lambda b, kv, i: (SW_K0 + kv, 0, b)),
            pl.BlockSpec((None, None, seq, HEAD_DIM), lambda b, kv, i: (SW_V0 + kv, b, 0, 0)),
            pl.BlockSpec((None, 3, rows, span), lambda b, kv, i: (kv, 0, 0, 0)),
            pl.BlockSpec((None, rows, HEAD_DIM), lambda b, kv, i: (kv, 0, 0)),
        ],
        out_specs=pl.BlockSpec((group, None, tq, HEAD_DIM), lambda b, kv, i: (kv, b, i, 0)),
        scratch_shapes=[
            pltpu.VMEM((seq, 2 * HEAD_DIM), BF16),
            pltpu.VMEM((rows, span), F32),
            pltpu.VMEM((rows, span), F32),
        ],
        compiler_params=_params(("parallel", "parallel", "arbitrary")),
        name="sw",
    )(qv, kt, qv, bias, sink)


def _merge_kernel(x_ref, g_ref, oa_ref, ob_ref, oc_ref, wg_ref, wb_ref, wo_ref, o_ref, h_ref, merged_ref):
    j = pl.program_id(1)

    @pl.when(j == 0)
    def _():
        h_ref[...] = _rms(x_ref[...], g_ref[...]).astype(BF16)

    h = h_ref[...]
    tn = wg_ref.shape[-1]
    cols = pl.ds(pl.multiple_of(j * tn, tn), tn)
    merged = None
    for n, br_ref in enumerate((oa_ref, ob_ref, oc_ref)):
        branch = jnp.concatenate([br_ref[hd] for hd in range(br_ref.shape[0])], axis=1)
        term = jax.nn.sigmoid(_dot(h, wg_ref[n])) * _dot(branch, wb_ref[n, :, cols])
        merged = term if merged is None else merged + term
    merged_ref[j] = merged.astype(BF16)

    @pl.when(j == pl.num_programs(1) - 1)
    def _():
        merged_all = jnp.concatenate([merged_ref[c] for c in range(merged_ref.shape[0])], axis=1)
        o_ref[...] = x_ref[...] + _dot(merged_all, wo_ref[...])


def _merge(x, g, oa, ob, oc, w_gate, w_branch, w_out):
    t, d = x.shape
    nh = oa.shape[0]
    bw = nh * HEAD_DIM
    tm = _pick_tile(t, 512)
    tn = _pick_tile(d, MERGE_TN)
    return pl.pallas_call(
        _merge_kernel,
        out_shape=jax.ShapeDtypeStruct((t, d), F32),
        grid=(t // tm, d // tn),
        in_specs=[
            pl.BlockSpec((tm, d), lambda i, j: (i, 0)),
            pl.BlockSpec((1, d), lambda i, j: (0, 0)),
            pl.BlockSpec((nh, tm, HEAD_DIM), lambda i, j: (0, i, 0)),
            pl.BlockSpec((nh, tm, HEAD_DIM), lambda i, j: (0, i, 0)),
            pl.BlockSpec((nh, tm, HEAD_DIM), lambda i, j: (0, i, 0)),
            pl.BlockSpec((3, d, tn), lambda i, j: (0, 0, j)),
            pl.BlockSpec((3, bw, d), lambda i, j: (0, 0, 0), pipeline_mode=pl.Buffered(1)),
            pl.BlockSpec((d, d), lambda i, j: (0, 0), pipeline_mode=pl.Buffered(1)),
        ],
        out_specs=pl.BlockSpec((tm, d), lambda i, j: (i, 0)),
        scratch_shapes=[pltpu.VMEM((tm, d), BF16), pltpu.VMEM((d // tn, tm, tn), BF16)],
        compiler_params=pltpu.CompilerParams(dimension_semantics=("parallel", "arbitrary"),
                                             vmem_limit_bytes=WIDE_VMEM_LIMIT_BYTES),
        name="merge",
    )(x, g, oa, ob, oc, w_gate, w_branch, w_out)


def _rope_tables(seq):
    quarter = HEAD_DIM // 4
    freqs = ROPE_THETA ** (-jnp.arange(quarter, dtype=F32) / quarter)
    t = jnp.arange(seq)
    row = (t // GRID_W).astype(F32)
    col = (t % GRID_W).astype(F32)
    ang_r = row[:, None] * freqs[None, :]
    ang_c = col[:, None] * freqs[None, :]
    cos = jnp.concatenate([jnp.cos(ang_r)] * 2 + [jnp.cos(ang_c)] * 2, axis=-1)
    sin = jnp.concatenate([-jnp.sin(ang_r), jnp.sin(ang_r), -jnp.sin(ang_c), jnp.sin(ang_c)], axis=-1)
    return cos, sin


def _gain_row(na_q, na_k, ga_q, ga_k, sw_q, sw_k):
    scale = HEAD_DIM ** -0.5 * LOG2E
    one = jnp.ones((HEAD_DIM,), F32)
    parts = ([na_q * scale] * 8 + [na_k] * 8 + [one] * 8 + [ga_q * scale] * 8 + [ga_k] * 2 + [one] * 2
             + [sw_q * scale] * 8 + [sw_k] * 2 + [one] * 2)
    return jnp.concatenate([p.astype(F32) for p in parts]).reshape(1, D_IN)


def _ga_shift(ga_q, ga_k):
    fold = HEAD_DIM ** -0.5 * LOG2E
    bound = (HEAD_DIM * fold * 1.02 * jnp.max(jnp.abs(ga_q.astype(F32)))
             * jnp.max(jnp.abs(ga_k.astype(F32))))
    return jnp.zeros((1,), F32), 2.0 * bound <= MAX_EXP2_RANGE


def _sw_sink_rows(sink, group):
    per_head = (sink.astype(F32) * LOG2E).reshape(SW_KV_HEADS, group, 1, 1)
    return jnp.broadcast_to(per_head, (SW_KV_HEADS, group, Q_BLOCK, HEAD_DIM)).reshape(
        SW_KV_HEADS, group * Q_BLOCK, HEAD_DIM)


def _trunk(x3, layers, rope):
    batch, seq, d = x3.shape
    x = x3.reshape(batch * seq, d)
    cos, sin = rope
    for p in layers:
        x = _ffn(x, p["ffn1_norm"], p["ffn1_wg"], p["ffn1_wu"], p["ffn1_wd"])
        qv, kt = _qkv(x, p["mix_norm"], p["w_in"], p["gain"], cos, sin, seq)
        qv = qv.reshape(N_QV, batch, seq, HEAD_DIM)
        shift, shift_ok = p["ga_shift"]
        oa = _na(qv, kt, p["na_bias"], batch, seq)
        ob = lax.cond(shift_ok,
                      lambda: _ga(shift, qv, kt, batch, seq, True),
                      lambda: _ga(shift, qv, kt, batch, seq, False))
        oc = _sw(qv, kt, p["sw_bias"], p["sw_sink"], batch, seq)
        oa = oa.reshape(NA_HEADS, batch * seq, HEAD_DIM)
        ob = ob.reshape(GA_HEADS, batch * seq, HEAD_DIM)
        oc = oc.reshape(SW_HEADS, batch * seq, HEAD_DIM)
        x = _merge(x, p["mix_norm"], oa, ob, oc, p["w_gate"], p["w_branch"], p["w_out"])
        x = _ffn(x, p["ffn2_norm"], p["ffn2_wg"], p["ffn2_wu"], p["ffn2_wd"])
    return x.reshape(batch, seq, d)


def kernel(x_prompt, x_sample, ffn1_norm, ffn1_wg, ffn1_wu, ffn1_wd, mix_norm, w_in, w_gate, na_q_norm, na_k_norm, na_rel_bias, ga_q_norm, ga_k_norm, sw_q_norm, sw_k_norm, sw_sink, w_branch, w_out, ffn2_norm, ffn2_wg, ffn2_wu, ffn2_wd):
    depth = w_in.shape[0]
    d = x_prompt.shape[-1]
    sw_group = SW_HEADS // SW_KV_HEADS
    sw_bias = _sw_bias_table()
    layers = []
    for l in range(depth):
        layers.append(dict(
            ffn1_norm=ffn1_norm[l].reshape(1, d).astype(F32),
            ffn1_wg=ffn1_wg[l].astype(BF16), ffn1_wu=ffn1_wu[l].astype(BF16), ffn1_wd=ffn1_wd[l].astype(BF16),
            mix_norm=mix_norm[l].reshape(1, d).astype(F32),
            w_in=w_in[l].astype(BF16),
            gain=_gain_row(na_q_norm[l], na_k_norm[l], ga_q_norm[l], ga_k_norm[l], sw_q_norm[l], sw_k_norm[l]),
            na_bias=_na_bias_table(na_rel_bias[l]),
            ga_shift=_ga_shift(ga_q_norm[l], ga_k_norm[l]),
            sw_bias=sw_bias,
            sw_sink=_sw_sink_rows(sw_sink[l], sw_group),
            w_gate=w_gate[l].astype(BF16), w_branch=w_branch[l].astype(BF16), w_out=w_out[l].astype(BF16),
            ffn2_norm=ffn2_norm[l].reshape(1, d).astype(F32),
            ffn2_wg=ffn2_wg[l].astype(BF16), ffn2_wu=ffn2_wu[l].astype(BF16), ffn2_wd=ffn2_wd[l].astype(BF16),
        ))
    outs = []
    for x3 in (x_prompt, x_sample):
        outs.append(_trunk(x3, layers, _rope_tables(x3.shape[1])))
    return tuple(outs)
```

```python
import functools
import math

import numpy as np
import jax
import jax.numpy as jnp
from jax import lax
from jax.experimental import pallas as pl
from jax.experimental.pallas import tpu as pltpu

F32 = jnp.float32
BF16 = jnp.bfloat16

EPS = 1e-6
HEAD_DIM = 128
GRID_W = 64
NA_HEADS = 8
NA_ROWS = 8
NA_COLS = 16
GA_HEADS = 8
GA_KV_HEADS = 2
SW_HEADS = 8
SW_KV_HEADS = 2
WINDOW = 128
Q_BLOCK = 128
ROPE_THETA = 10000.0
MASKED = -1e30
LOG2E = math.log2(math.e)
MAX_EXP2_RANGE = 100.0

N_IN_HEADS = 48
D_IN = N_IN_HEADS * HEAD_DIM

NA_Q0, GA_Q0, SW_Q0, NA_V0, GA_V0, SW_V0 = 0, 8, 16, 24, 32, 34
N_QV = 36
NA_K0, GA_K0, SW_K0 = 0, 8, 10
N_KT = 12

QKV_SEGMENTS = (
    (0, 8, True, False, False, NA_Q0), (8, 8, True, False, True, NA_K0), (16, 8, False, False, False, NA_V0),
    (24, 8, True, True, False, GA_Q0), (32, 2, True, True, True, GA_K0), (34, 2, False, False, False, GA_V0),
    (36, 8, True, False, False, SW_Q0), (44, 2, True, False, True, SW_K0), (46, 2, False, False, False, SW_V0),
)

NA_QROWS = 8
NA_KROWS = 16
NA_TQ = NA_QROWS * GRID_W
NA_TK = NA_KROWS * GRID_W

VMEM_LIMIT_BYTES = 56 * 1024 * 1024
FFN_TM = 1024
FFN_TF = 512
FFN_VMEM_LIMIT_BYTES = 60 * 1024 * 1024
MERGE_TN = 256


def _params(sem):
    return pltpu.CompilerParams(dimension_semantics=sem, vmem_limit_bytes=VMEM_LIMIT_BYTES)


def _rms(xf, g):
    ms = jnp.mean(xf * xf, axis=-1, keepdims=True)
    return xf * lax.rsqrt(ms + EPS) * g


def _dot(a, b):
    return jnp.dot(a, b, preferred_element_type=F32)


def _pick_tile(n, want):
    t = min(n, want)
    while n % t:
        t //= 2
    return t


def _ffn_kernel(x_ref, g_ref, wg_ref, wu_ref, wd_ref, o_ref, h_ref):
    j = pl.program_id(1)

    def down():
        h = h_ref[...]
        a = _dot(h, wg_ref[...])
        u = _dot(h, wu_ref[...])
        act = (a * jax.nn.sigmoid(a) * u).astype(BF16)
        return _dot(act, wd_ref[...])

    @pl.when(j == 0)
    def _():
        h_ref[...] = _rms(x_ref[...], g_ref[...]).astype(BF16)
        o_ref[...] = down()

    @pl.when(j > 0)
    def _():
        o_ref[...] += down()

    @pl.when(j == pl.num_programs(1) - 1)
    def _():
        o_ref[...] = x_ref[...] + 0.5 * o_ref[...]


def _ffn(x, g, wg, wu, wd):
    t, d = x.shape
    f = wg.shape[1]
    tm = _pick_tile(t, FFN_TM)
    tf = _pick_tile(f, FFN_TF)
    return pl.pallas_call(
        _ffn_kernel,
        out_shape=jax.ShapeDtypeStruct((t, d), F32),
        grid=(t // tm, f // tf),
        in_specs=[
            pl.BlockSpec((tm, d), lambda i, j: (i, 0)),
            pl.BlockSpec((1, d), lambda i, j: (0, 0)),
            pl.BlockSpec((d, tf), lambda i, j: (0, j)),
            pl.BlockSpec((d, tf), lambda i, j: (0, j)),
            pl.BlockSpec((tf, d), lambda i, j: (j, 0)),
        ],
        out_specs=pl.BlockSpec((tm, d), lambda i, j: (i, 0)),
        scratch_shapes=[pltpu.VMEM((tm, d), BF16)],
        compiler_params=pltpu.CompilerParams(dimension_semantics=("parallel", "arbitrary"),
                                             vmem_limit_bytes=FFN_VMEM_LIMIT_BYTES),
        name="ffn",
    )(x, g, wg, wu, wd)


def _swap_rotary_halves(y):
    lane = lax.broadcasted_iota(jnp.int32, y.shape, 1)
    first = (lane & 32) == 0
    return jnp.where(first, pltpu.roll(y, 96, 1), pltpu.roll(y, 32, 1))


def _qkv_kernel(x_ref, g_ref, w_ref, gain_ref, cos_ref, sin_ref, qv_ref, kt_ref, h_ref):
    h = _rms(x_ref[...], g_ref[...]).astype(BF16)
    h_ref[...] = h
    for h0, nh, norm, rope, is_key, slot0 in QKV_SEGMENTS:
        c0 = h0 * HEAD_DIM
        y = _dot(h, w_ref[:, c0:c0 + nh * HEAD_DIM])
        for hh in range(nh):
            lo = hh * HEAD_DIM
            yh = y[:, lo:lo + HEAD_DIM]
            if norm:
                yh = _rms(yh, gain_ref[:, c0 + lo:c0 + lo + HEAD_DIM])
            if rope:
                yh = yh * cos_ref[...] + _swap_rotary_halves(yh) * sin_ref[...]
            if is_key:
                kt_ref[slot0 + hh] = yh.T.astype(BF16)
            else:
                qv_ref[slot0 + hh] = yh.astype(BF16)


def _qkv(x, g, w_in, gain, cos, sin, seq):
    t, d = x.shape
    tm = _pick_tile(seq, 256)
    nseq = seq // tm
    return pl.pallas_call(
        _qkv_kernel,
        out_shape=(jax.ShapeDtypeStruct((N_QV, t, HEAD_DIM), BF16),
                   jax.ShapeDtypeStruct((N_KT, HEAD_DIM, t), BF16),
                   jax.ShapeDtypeStruct((t, d), BF16)),
        grid=(t // tm,),
        in_specs=[
            pl.BlockSpec((tm, d), lambda i: (i, 0)),
            pl.BlockSpec((1, d), lambda i: (0, 0)),
            pl.BlockSpec((d, D_IN), lambda i: (0, 0), pipeline_mode=pl.Buffered(1)),
            pl.BlockSpec((1, D_IN), lambda i: (0, 0)),
            pl.BlockSpec((tm, HEAD_DIM), lambda i: (i % nseq, 0)),
            pl.BlockSpec((tm, HEAD_DIM), lambda i: (i % nseq, 0)),
        ],
        out_specs=(pl.BlockSpec((N_QV, tm, HEAD_DIM), lambda i: (0, i, 0)),
                   pl.BlockSpec((N_KT, HEAD_DIM, tm), lambda i: (0, 0, i)),
                   pl.BlockSpec((tm, d), lambda i: (i, 0))),
        compiler_params=_params(("parallel",)),
        name="qkv",
    )(x, g, w_in, gain, cos, sin)


def _edge_variant(i, nblk):
    return jnp.where(i == 0, 0, jnp.where(i == nblk - 1, 2, 1))


def _two_stage_loop(nblk, logits, finish, sa_ref, sb_ref):
    logits(0, sa_ref)

    def body(j, carry):
        i = 2 * j
        logits(i + 1, sb_ref)
        finish(i, sa_ref)
        logits(i + 2, sa_ref)
        finish(i + 1, sb_ref)
        return carry

    lax.fori_loop(0, nblk // 2 - 1, body, 0)
    logits(nblk - 1, sb_ref)
    finish(nblk - 2, sa_ref)
    finish(nblk - 1, sb_ref)


def _na_kernel(q_ref, kt_ref, v_ref, bias_ref, o_ref, v1_ref, sa_ref, sb_ref, *, rows):
    nblk = q_ref.shape[0] // NA_TQ
    v1_ref[:, :HEAD_DIM] = v_ref[...]
    v1_ref[:, HEAD_DIM:] = jnp.ones(v_ref.shape, BF16)

    def window(i):
        start_row = jnp.clip(i * NA_QROWS - NA_ROWS // 2, 0, rows - NA_KROWS)
        return pl.multiple_of(start_row * GRID_W, (NA_ROWS // 2) * GRID_W)

    def logits(i, dst_ref):
        q = q_ref[pl.ds(pl.multiple_of(i * NA_TQ, NA_TQ), NA_TQ), :]
        dst_ref[...] = _dot(q, kt_ref[:, pl.ds(window(i), NA_TK)])

    def finish(i, src_ref):
        s = src_ref[...] + bias_ref[_edge_variant(i, nblk)]
        m = jnp.max(s, axis=-1, keepdims=True)
        p = jnp.exp2(s - m).astype(BF16)
        pv = _dot(p, v1_ref[pl.ds(window(i), NA_TK), :])
        o = pv[:, :HEAD_DIM] / pv[:, HEAD_DIM:]
        o_ref[pl.ds(pl.multiple_of(i * NA_TQ, NA_TQ), NA_TQ), :] = o.astype(BF16)

    _two_stage_loop(nblk, logits, finish, sa_ref, sb_ref)


def _na(qv, kt, bias, batch, seq):
    rows = seq // GRID_W
    assert seq % (2 * NA_TQ) == 0 and rows >= NA_KROWS
    return pl.pallas_call(
        functools.partial(_na_kernel, rows=rows),
        out_shape=jax.ShapeDtypeStruct((NA_HEADS, batch, seq, HEAD_DIM), BF16),
        grid=(NA_HEADS, batch),
        in_specs=[
            pl.BlockSpec((None, None, seq, HEAD_DIM), lambda h, b: (NA_Q0 + h, b, 0, 0)),
            pl.BlockSpec((None, HEAD_DIM, seq), lambda h, b: (NA_K0 + h, 0, b)),
            pl.BlockSpec((None, None, seq, HEAD_DIM), lambda h, b: (NA_V0 + h, b, 0, 0)),
            pl.BlockSpec((None, 3, NA_TQ, NA_TK), lambda h, b: (h, 0, 0, 0)),
        ],
        out_specs=pl.BlockSpec((None, None, seq, HEAD_DIM), lambda h, b: (h, b, 0, 0)),
        scratch_shapes=[
            pltpu.VMEM((seq, 2 * HEAD_DIM), BF16),
            pltpu.VMEM((NA_TQ, NA_TK), F32),
            pltpu.VMEM((NA_TQ, NA_TK), F32),
        ],
        compiler_params=_params(("parallel", "parallel")),
        name="na",
    )(qv, kt, qv, bias)


def _na_bias_table(rel_bias):
    rel = rel_bias.astype(F32) * LOG2E
    rel_pad = jnp.pad(rel, ((0, 0), (0, 0), (GRID_W, GRID_W)))
    off = GRID_W + NA_COLS - 1
    t1 = jnp.stack([rel_pad[:, :, off - c:off - c + GRID_W] for c in range(GRID_W)], axis=1)
    cq = np.arange(GRID_W)[:, None]
    kc = np.arange(GRID_W)[None, :]
    cs = np.clip(cq - NA_COLS // 2, 0, GRID_W - NA_COLS)
    col_ok = (kc >= cs) & (kc < cs + NA_COLS)
    t1 = jnp.where(col_ok[None, :, None, :], t1, MASKED)

    rows = 3 * NA_QROWS
    blocks = []
    for v in range(3):
        start_row = int(np.clip(v * NA_QROWS - NA_ROWS // 2, 0, rows - NA_KROWS))
        for qr in range(NA_QROWS):
            r = v * NA_QROWS + qr
            rs = int(np.clip(r - NA_ROWS // 2, 0, rows - NA_ROWS))
            lo = rs - start_row
            dr0 = rs - r + NA_ROWS - 1
            blk = t1[:, :, dr0:dr0 + NA_ROWS, :]
            blocks.append(jnp.pad(blk, ((0, 0), (0, 0), (lo, NA_KROWS - NA_ROWS - lo), (0, 0)),
                                  constant_values=MASKED))
    table = jnp.stack(blocks, axis=1)
    return table.reshape(NA_HEADS, 3, NA_TQ, NA_TK)


def _ga_kernel(bound_ref, q_ref, kt_ref, v_ref, o_ref, v1_ref, m_ref, acc_ref, sa_ref, sb_ref,
               *, tq, tk, group, bounded):
    @pl.when(pl.program_id(2) == 0)
    def _():
        v1_ref[:, :HEAD_DIM] = v_ref[...]
        v1_ref[:, HEAD_DIM:] = jnp.ones(v_ref.shape, BF16)

    m_ref[...] = jnp.full(m_ref.shape, MASKED, F32)
    acc_ref[...] = jnp.zeros(acc_ref.shape, F32)
    lane_tiles = tk // HEAD_DIM
    n_chunks = kt_ref.shape[1] // tk

    def logits(c, dst_ref):
        start = pl.multiple_of(c * tk, tk)
        q = q_ref[...].reshape(group * tq, HEAD_DIM)
        s = _dot(q, kt_ref[:, pl.ds(start, tk)])
        if bounded:
            dst_ref[...] = jnp.exp2(s - bound_ref[0]).astype(BF16)
        else:
            dst_ref[...] = s

    def accumulate(c, src_ref):
        start = pl.multiple_of(c * tk, tk)
        if bounded:
            acc_ref[...] += _dot(src_ref[...], v1_ref[pl.ds(start, tk), :])
            return
        s = src_ref[...]
        m_prev = m_ref[...]
        m_new = jnp.maximum(m_prev, jnp.max(s, axis=-1, keepdims=True))
        alpha = jnp.exp2(m_prev - m_new)
        p = jnp.exp2(s - jnp.concatenate([m_new] * lane_tiles, axis=1))
        pv = _dot(p.astype(BF16), v1_ref[pl.ds(start, tk), :])
        acc_ref[...] = jnp.concatenate([alpha, alpha], axis=1) * acc_ref[...] + pv
        m_ref[...] = m_new

    _two_stage_loop(n_chunks, logits, accumulate, sa_ref, sb_ref)
    o = acc_ref[:, :HEAD_DIM] / acc_ref[:, HEAD_DIM:]
    o_ref[...] = o.reshape(group, tq, HEAD_DIM).astype(BF16)


def _ga(bound, qv, kt, batch, seq, bounded):
    group = GA_HEADS // GA_KV_HEADS
    tq = _pick_tile(seq, 256)
    tk = _pick_tile(seq, 512)
    assert (seq // tk) % 2 == 0
    score_dtype = BF16 if bounded else F32
    return pl.pallas_call(
        functools.partial(_ga_kernel, tq=tq, tk=tk, group=group, bounded=bounded),
        out_shape=jax.ShapeDtypeStruct((GA_HEADS, batch, seq, HEAD_DIM), BF16),
        grid=(batch, GA_KV_HEADS, seq // tq),
        in_specs=[
            pl.BlockSpec(memory_space=pltpu.SMEM),
            pl.BlockSpec((group, None, tq, HEAD_DIM), lambda b, kv, i: (GA_Q0 // group + kv, b, i, 0)),
            pl.BlockSpec((None, HEAD_DIM, seq), lambda b, kv, i: (GA_K0 + kv, 0, b)),
            pl.BlockSpec((None, None, seq, HEAD_DIM), lambda b, kv, i: (GA_V0 + kv, b, 0, 0)),
        ],
        out_specs=pl.BlockSpec((group, None, tq, HEAD_DIM), lambda b, kv, i: (kv, b, i, 0)),
        scratch_shapes=[
            pltpu.VMEM((seq, 2 * HEAD_DIM), BF16),
            pltpu.VMEM((group * tq, HEAD_DIM), F32),
            pltpu.VMEM((group * tq, 2 * HEAD_DIM), F32),
            pltpu.VMEM((group * tq, tk), score_dtype),
            pltpu.VMEM((group * tq, tk), score_dtype),
        ],
        compiler_params=_params(("parallel", "parallel", "arbitrary")),
        name="ga_bounded" if bounded else "ga",
    )(bound, qv, kt, qv)


def _sw_kernel(q_ref, kt_ref, v_ref, bias_ref, sink_ref, o_ref, v1_ref, sa_ref, sb_ref, *, seq, group):
    span = Q_BLOCK + 2 * WINDOW
    nblk_seq = seq // Q_BLOCK
    nblk = q_ref.shape[1] // Q_BLOCK
    first = pl.program_id(2) * nblk

    @pl.when(pl.program_id(2) == 0)
    def _():
        v1_ref[:, :HEAD_DIM] = v_ref[...]
        v1_ref[:, HEAD_DIM:] = jnp.ones(v_ref.shape, BF16)

    def window(n):
        return pl.multiple_of(jnp.clip((first + n) * Q_BLOCK - WINDOW, 0, seq - span), Q_BLOCK)

    def rows(n):
        return pl.ds(pl.multiple_of(n * Q_BLOCK, Q_BLOCK), Q_BLOCK)

    def logits(n, dst_ref):
        q = q_ref[:, rows(n), :].reshape(group * Q_BLOCK, HEAD_DIM)
        dst_ref[...] = _dot(q, kt_ref[:, pl.ds(window(n), span)])

    def finish(n, src_ref):
        sink = sink_ref[...]
        s = src_ref[...] + bias_ref[_edge_variant(first + n, nblk_seq)]
        m = jnp.maximum(jnp.max(s, axis=-1, keepdims=True), sink)
        p = jnp.exp2(s - jnp.concatenate([m] * (span // HEAD_DIM), axis=1)).astype(BF16)
        pv = _dot(p, v1_ref[pl.ds(window(n), span), :])
        o = pv[:, :HEAD_DIM] / (pv[:, HEAD_DIM:] + jnp.exp2(sink - m))
        o_ref[:, rows(n), :] = o.reshape(group, Q_BLOCK, HEAD_DIM).astype(BF16)

    _two_stage_loop(nblk, logits, finish, sa_ref, sb_ref)


def _sw_bias_table():
    group = SW_HEADS // SW_KV_HEADS
    span = Q_BLOCK + 2 * WINDOW
    slopes = 2.0 ** (-8.0 * np.arange(1, SW_HEADS + 1, dtype=np.float64) / SW_HEADS)
    qi = np.arange(Q_BLOCK)[:, None]
    kj = np.arange(span)[None, :]
    table = np.empty((SW_KV_HEADS, 3, group, Q_BLOCK, span), np.float32)
    for v in range(3):
        dist = np.abs(qi - (kj - v * WINDOW))
        for h in range(SW_HEADS):
            table[h // group, v, h % group] = np.where(dist <= WINDOW, -LOG2E * slopes[h] * dist, MASKED)
    return jnp.asarray(table.reshape(SW_KV_HEADS, 3, group * Q_BLOCK, span))


def _sw(qv, kt, bias, sink, batch, seq):
    group = SW_HEADS // SW_KV_HEADS
    span = Q_BLOCK + 2 * WINDOW
    tq = _pick_tile(seq, 16 * Q_BLOCK)
    assert tq % (2 * Q_BLOCK) == 0 and seq >= span
    rows = group * Q_BLOCK
    return pl.pallas_call(
        functools.partial(_sw_kernel, seq=seq, group=group),
        out_shape=jax.ShapeDtypeStruct((SW_HEADS, batch, seq, HEAD_DIM), BF16),
        grid=(batch, SW_KV_HEADS, seq // tq),
        in_specs=[
            pl.BlockSpec((group, None, tq, HEAD_DIM), lambda b, kv, i: (SW_Q0 // group + kv, b, i, 0)),
            pl.BlockSpec((None, HEAD_DIM, seq), lambda b, kv, i: (SW_K0 + kv, 0, b)),
            pl.BlockSpec((None, None, seq, HEAD_DIM), lambda b, kv, i: (SW_V0 + kv, b, 0, 0)),
            pl.BlockSpec((None, 3, rows, span), lambda b, kv, i: (kv, 0, 0, 0)),
            pl.BlockSpec((None, rows, HEAD_DIM), lambda b, kv, i: (kv, 0, 0)),
        ],
        out_specs=pl.BlockSpec((group, None, tq, HEAD_DIM), lambda b, kv, i: (kv, b, i, 0)),
        scratch_shapes=[
            pltpu.VMEM((seq, 2 * HEAD_DIM), BF16),
            pltpu.VMEM((rows, span), F32),
            pltpu.VMEM((rows, span), F32),
        ],
        compiler_params=_params(("parallel", "parallel", "arbitrary")),
        name="sw",
    )(qv, kt, qv, bias, sink)


def _merge_kernel(x_ref, h_ref, oa_ref, ob_ref, oc_ref, wg_ref, wb_ref, wo_ref, o_ref, merged_ref):
    j = pl.program_id(1)
    h = h_ref[...]
    merged = None
    for n, br_ref in enumerate((oa_ref, ob_ref, oc_ref)):
        branch = jnp.concatenate([br_ref[hd] for hd in range(br_ref.shape[0])], axis=1)
        term = jax.nn.sigmoid(_dot(h, wg_ref[n])) * _dot(branch, wb_ref[n])
        merged = term if merged is None else merged + term
    merged_ref[j] = merged.astype(BF16)

    @pl.when(j == pl.num_programs(1) - 1)
    def _():
        merged_all = jnp.concatenate([merged_ref[c] for c in range(merged_ref.shape[0])], axis=1)
        o_ref[...] = x_ref[...] + _dot(merged_all, wo_ref[...])


def _merge(x, h, oa, ob, oc, w_gate, w_branch, w_out):
    t, d = x.shape
    nh = oa.shape[0]
    bw = nh * HEAD_DIM
    tm = _pick_tile(t, 512)
    tn = _pick_tile(d, MERGE_TN)
    return pl.pallas_call(
        _merge_kernel,
        out_shape=jax.ShapeDtypeStruct((t, d), F32),
        grid=(t // tm, d // tn),
        in_specs=[
            pl.BlockSpec((tm, d), lambda i, j: (i, 0)),
            pl.BlockSpec((tm, d), lambda i, j: (i, 0)),
            pl.BlockSpec((nh, tm, HEAD_DIM), lambda i, j: (0, i, 0)),
            pl.BlockSpec((nh, tm, HEAD_DIM), lambda i, j: (0, i, 0)),
            pl.BlockSpec((nh, tm, HEAD_DIM), lambda i, j: (0, i, 0)),
            pl.BlockSpec((3, d, tn), lambda i, j: (0, 0, j)),
            pl.BlockSpec((3, bw, tn), lambda i, j: (0, 0, j)),
            pl.BlockSpec((d, d), lambda i, j: (0, 0), pipeline_mode=pl.Buffered(1)),
        ],
        out_specs=pl.BlockSpec((tm, d), lambda i, j: (i, 0)),
        scratch_shapes=[pltpu.VMEM((d // tn, tm, tn), BF16)],
        compiler_params=_params(("parallel", "arbitrary")),
        name="merge",
    )(x, h, oa, ob, oc, w_gate, w_branch, w_out)


def _rope_tables(seq):
    quarter = HEAD_DIM // 4
    freqs = ROPE_THETA ** (-jnp.arange(quarter, dtype=F32) / quarter)
    t = jnp.arange(seq)
    row = (t // GRID_W).astype(F32)
    col = (t % GRID_W).astype(F32)
    ang_r = row[:, None] * freqs[None, :]
    ang_c = col[:, None] * freqs[None, :]
    cos = jnp.concatenate([jnp.cos(ang_r)] * 2 + [jnp.cos(ang_c)] * 2, axis=-1)
    sin = jnp.concatenate([-jnp.sin(ang_r), jnp.sin(ang_r), -jnp.sin(ang_c), jnp.sin(ang_c)], axis=-1)
    return cos, sin


def _gain_row(na_q, na_k, ga_q, ga_k, sw_q, sw_k):
    scale = HEAD_DIM ** -0.5 * LOG2E
    one = jnp.ones((HEAD_DIM,), F32)
    parts = ([na_q * scale] * 8 + [na_k] * 8 + [one] * 8 + [ga_q * scale] * 8 + [ga_k] * 2 + [one] * 2
             + [sw_q * scale] * 8 + [sw_k] * 2 + [one] * 2)
    return jnp.concatenate([p.astype(F32) for p in parts]).reshape(1, D_IN)


def _ga_shift(ga_q, ga_k):
    fold = HEAD_DIM ** -0.5 * LOG2E
    bound = (HEAD_DIM * fold * 1.02 * jnp.max(jnp.abs(ga_q.astype(F32)))
             * jnp.max(jnp.abs(ga_k.astype(F32))))
    return jnp.zeros((1,), F32), 2.0 * bound <= MAX_EXP2_RANGE


def _sw_sink_rows(sink, group):
    per_head = (sink.astype(F32) * LOG2E).reshape(SW_KV_HEADS, group, 1, 1)
    return jnp.broadcast_to(per_head, (SW_KV_HEADS, group, Q_BLOCK, HEAD_DIM)).reshape(
        SW_KV_HEADS, group * Q_BLOCK, HEAD_DIM)


def _trunk(x3, layers, rope):
    batch, seq, d = x3.shape
    x = x3.reshape(batch * seq, d)
    cos, sin = rope
    for p in layers:
        x = _ffn(x, p["ffn1_norm"], p["ffn1_wg"], p["ffn1_wu"], p["ffn1_wd"])
        qv, kt, h = _qkv(x, p["mix_norm"], p["w_in"], p["gain"], cos, sin, seq)
        qv = qv.reshape(N_QV, batch, seq, HEAD_DIM)
        shift, shift_ok = p["ga_shift"]
        oa = _na(qv, kt, p["na_bias"], batch, seq)
        ob = lax.cond(shift_ok,
                      lambda: _ga(shift, qv, kt, batch, seq, True),
                      lambda: _ga(shift, qv, kt, batch, seq, False))
        oc = _sw(qv, kt, p["sw_bias"], p["sw_sink"], batch, seq)
        oa = oa.reshape(NA_HEADS, batch * seq, HEAD_DIM)
        ob = ob.reshape(GA_HEADS, batch * seq, HEAD_DIM)
        oc = oc.reshape(SW_HEADS, batch * seq, HEAD_DIM)
        x = _merge(x, h, oa, ob, oc, p["w_gate"], p["w_branch"], p["w_out"])
        x = _ffn(x, p["ffn2_norm"], p["ffn2_wg"], p["ffn2_wu"], p["ffn2_wd"])
    return x.reshape(batch, seq, d)


def kernel(x_prompt, x_sample, ffn1_norm, ffn1_wg, ffn1_wu, ffn1_wd, mix_norm, w_in, w_gate, na_q_norm, na_k_norm, na_rel_bias, ga_q_norm, ga_k_norm, sw_q_norm, sw_k_norm, sw_sink, w_branch, w_out, ffn2_norm, ffn2_wg, ffn2_wu, ffn2_wd):
    depth = w_in.shape[0]
    d = x_prompt.shape[-1]
    sw_group = SW_HEADS // SW_KV_HEADS
    sw_bias = _sw_bias_table()
    layers = []
    for l in range(depth):
        layers.append(dict(
            ffn1_norm=ffn1_norm[l].reshape(1, d).astype(F32),
            ffn1_wg=ffn1_wg[l].astype(BF16), ffn1_wu=ffn1_wu[l].astype(BF16), ffn1_wd=ffn1_wd[l].astype(BF16),
            mix_norm=mix_norm[l].reshape(1, d).astype(F32),
            w_in=w_in[l].astype(BF16),
            gain=_gain_row(na_q_norm[l], na_k_norm[l], ga_q_norm[l], ga_k_norm[l], sw_q_norm[l], sw_k_norm[l]),
            na_bias=_na_bias_table(na_rel_bias[l]),
            ga_shift=_ga_shift(ga_q_norm[l], ga_k_norm[l]),
            sw_bias=sw_bias,
            sw_sink=_sw_sink_rows(sw_sink[l], sw_group),
            w_gate=w_gate[l].astype(BF16), w_branch=w_branch[l].astype(BF16), w_out=w_out[l].astype(BF16),
            ffn2_norm=ffn2_norm[l].reshape(1, d).astype(F32),
            ffn2_wg=ffn2_wg[l].astype(BF16), ffn2_wu=ffn2_wu[l].astype(BF16), ffn2_wd=ffn2_wd[l].astype(BF16),
        ))
    outs = []
    for x3 in (x_prompt, x_sample):
        outs.append(_trunk(x3, layers, _rope_tables(x3.shape[1])))
    return tuple(outs)
```
